```python
import math
import jax, jax.numpy as jnp
from jax import lax
import numpy as np

D_MODEL = 1024
BATCH = 16
SEQ = 2048
DEPTH = 4

N_MIXERS = 2
N_A = (DEPTH + 1) // 2
N_B = DEPTH // 2
RMS_EPS = 1e-6

A_GROUPS = ((128, 1), (512, 4), (2048, 16))
A_N_GROUPS = len(A_GROUPS)
A_HEADS_PER_GROUP = 8
A_HEAD_DIM = 64
A_HEADS = A_N_GROUPS * A_HEADS_PER_GROUP
A_QKV_COLS = 3 * A_HEADS * A_HEAD_DIM
A_MERGED = A_HEADS_PER_GROUP * A_HEAD_DIM
ROPE_THETA = 10000.0
NEG_INF = -1e30

B_HEADS = 4
B_KEY_DIM = D_MODEL // 2
B_VAL_DIM = D_MODEL
B_DK = B_KEY_DIM // B_HEADS
B_DV = B_VAL_DIM // B_HEADS
B_GATE_RANK = 16
B_GATE_NORMALIZER = 16.0
B_CHUNK = 64
B_IN_COLS = 2 * B_KEY_DIM + 2 * B_VAL_DIM + B_GATE_RANK

FFN_DIM = 2816
CONV_WIDTH = 3

kernel_name = "hybrid_dilated_attn_gla_convffn"


def rms_norm(x, g):
    xf = x.astype(jnp.float32)
    y = xf * lax.rsqrt(jnp.mean(xf * xf, axis=-1, keepdims=True) + RMS_EPS)
    return (y * g.astype(jnp.float32)).astype(x.dtype)


def rope_tables(positions, dim):
    inv = ROPE_THETA ** (-jnp.arange(0, dim, 2, dtype=jnp.float32) / dim)
    ang = positions.astype(jnp.float32)[..., None] * inv
    return jnp.cos(ang), jnp.sin(ang)


def apply_rope(x, cos, sin):
    x1, x2 = jnp.split(x.astype(jnp.float32), 2, axis=-1)
    c = cos[:, :, None, :]
    s = sin[:, :, None, :]
    return jnp.concatenate([x1 * c - x2 * s, x2 * c + x1 * s], axis=-1).astype(x.dtype)


def dilated_window_attention(q, k, v, span, dilation):
    B, S, H, dh = q.shape
    L = S // dilation
    nb = -(-L // span)
    Lp = nb * span

    def to_blocks(t):
        t = t.reshape(B, L, dilation, H, dh).transpose(0, 2, 1, 3, 4)
        t = jnp.pad(t, ((0, 0), (0, 0), (0, Lp - L), (0, 0), (0, 0)))
        return t.reshape(B, dilation, nb, span, H, dh)

    qb, kb, vb = to_blocks(q), to_blocks(k), to_blocks(v)

    def with_prev(t):
        prev = jnp.pad(t, ((0, 0), (0, 0), (1, 0), (0, 0), (0, 0), (0, 0)))[:, :, :-1]
        return jnp.concatenate([prev, t], axis=3)

    kk, vv = with_prev(kb), with_prev(vb)
    s = jnp.einsum('brnqhd,brnkhd->brnhqk', qb, kk,
                   preferred_element_type=jnp.float32)
    i = jnp.arange(span)[:, None]
    c = jnp.arange(2 * span)[None, :]
    dist = span + i - c
    band = (dist >= 0) & (dist <= span)
    not_first = (jnp.arange(nb) > 0)[:, None, None]
    mask = band[None] & (not_first | (c >= span)[None])
    s = jnp.where(mask[None, None, :, None], s, NEG_INF)
    lse = jax.nn.logsumexp(s, axis=-1)
    p = jnp.exp(s - lse[..., None])
    o = jnp.einsum('brnhqk,brnkhd->brnqhd', p.astype(v.dtype), vv)
    o = o.reshape(B, dilation, Lp, H, dh)[:, :, :L]
    o = o.transpose(0, 2, 1, 3, 4).reshape(B, S, H, dh)
    lse = lse.transpose(0, 1, 2, 4, 3).reshape(B, dilation, Lp, H)[:, :, :L]
    lse = lse.transpose(0, 2, 1, 3).reshape(B, S, H)
    return o, lse


def mixer_dilated(h, w_qkv, w_o, cos, sin):
    B, S, _ = h.shape
    qkv = (h @ w_qkv).reshape(B, S, 3, A_HEADS, A_HEAD_DIM)
    q = apply_rope(qkv[:, :, 0], cos, sin) * (A_HEAD_DIM ** -0.5)
    k = apply_rope(qkv[:, :, 1], cos, sin)
    v = qkv[:, :, 2]
    outs, lses = [], []
    for g, (window, dilation) in enumerate(A_GROUPS):
        sl = slice(g * A_HEADS_PER_GROUP, (g + 1) * A_HEADS_PER_GROUP)
        o, lse = dilated_window_attention(q[:, :, sl], k[:, :, sl], v[:, :, sl],
                                          window // dilation, dilation)
        outs.append(o)
        lses.append(lse)
    o = jnp.stack(outs, axis=0)
    lse = jnp.stack(lses, axis=0)
    wgt = jax.nn.softmax(lse, axis=0)
    merged = jnp.einsum('gbsh,gbshd->bshd', wgt.astype(o.dtype), o)
    return merged.reshape(B, S, A_MERGED) @ w_o


def gla_chunked(q, k, v, glog):
    B, S, H, dk = q.shape
    dv = v.shape[-1]
    C = B_CHUNK
    n = S // C

    def chunks(t):
        return t.astype(jnp.float32).reshape(B, n, C, H, -1).transpose(1, 0, 3, 2, 4)

    qc, kc, vc, gc = chunks(q), chunks(k), chunks(v), chunks(glog)
    b = jnp.cumsum(gc, axis=3)
    b_mid = b[:, :, :, C // 2 - 1:C // 2]
    b_last = b[:, :, :, -1:]
    a = jnp.einsum('nbhid,nbhjd->nbhij', qc * jnp.exp(b - b_mid), kc * jnp.exp(b_mid - b))
    causal = jnp.tril(jnp.ones((C, C), dtype=bool))
    a = jnp.where(causal, a, 0.0)
    o_intra = jnp.einsum('nbhij,nbhjv->nbhiv', a, vc)
    q_inter = qc * jnp.exp(b)
    k_state = kc * jnp.exp(b_last - b)
    decay = jnp.exp(b_last[:, :, :, 0])

    def step(state, xs):
        qi, ks, vi, dec = xs
        o = jnp.einsum('bhid,bhdv->bhiv', qi, state)
        state = dec[..., None] * state + jnp.einsum('bhjd,bhjv->bhdv', ks, vi)
        return state, o

    state0 = jnp.zeros((B, H, dk, dv), jnp.float32)
    _, o_inter = lax.scan(step, state0, (q_inter, k_state, vc, decay))
    o = o_intra + o_inter
    return o.transpose(1, 0, 3, 2, 4).reshape(B, S, H, dv)


def mixer_gla(h, w_in, w_gate_up, b_gate_up, g_norm, w_o):
    B, S, _ = h.shape
    proj = h @ w_in
    splits = [B_KEY_DIM, 2 * B_KEY_DIM, 2 * B_KEY_DIM + B_VAL_DIM, 2 * B_KEY_DIM + 2 * B_VAL_DIM]
    q, k, v, r, gdown = jnp.split(proj, splits, axis=-1)
    glog = jax.nn.log_sigmoid((gdown @ w_gate_up + b_gate_up).astype(jnp.float32)) / B_GATE_NORMALIZER
    q = q.reshape(B, S, B_HEADS, B_DK) * (B_DK ** -0.5)
    k = k.reshape(B, S, B_HEADS, B_DK)
    v = v.reshape(B, S, B_HEADS, B_DV)
    o = gla_chunked(q, k, v, glog.reshape(B, S, B_HEADS, B_DK))
    o = rms_norm(o, g_norm).astype(h.dtype).reshape(B, S, B_VAL_DIM)
    return (o * jax.nn.silu(r)) @ w_o


def conv_ffn(h, w_in, conv_w, conv_b, w_down):
    a, u = jnp.split(h @ w_in, 2, axis=-1)
    a = lax.conv_general_dilated(a, conv_w[:, None, :].astype(a.dtype), window_strides=(1,),
                                 padding=[(CONV_WIDTH - 1, 0)],
                                 dimension_numbers=('NWC', 'WIO', 'NWC'),
                                 feature_group_count=FFN_DIM) + conv_b
    return (jax.nn.silu(a) * u) @ w_down


def setup_inputs(seed: int = 0) -> dict:
    key = jax.random.key(seed)
    ks = jax.random.split(key, 20)
    f32 = jnp.float32
    res_scale = (2.0 * DEPTH) ** -0.5

    def nrm(k, shape, scale):
        return jax.random.normal(k, shape, f32) * scale

    x = jax.random.normal(ks[0], (BATCH, SEQ, D_MODEL), f32)
    positions = jnp.broadcast_to(jnp.arange(SEQ, dtype=jnp.int32), (BATCH, SEQ))
    return {
        'x': x,
        'positions': positions,
        'norm_mix': 1.0 + nrm(ks[1], (DEPTH, D_MODEL), 0.02),
        'norm_ffn': 1.0 + nrm(ks[2], (DEPTH, D_MODEL), 0.02),
        'a_w_qkv': nrm(ks[3], (N_A, D_MODEL, A_QKV_COLS), D_MODEL ** -0.5),
        'a_w_o': nrm(ks[4], (N_A, A_MERGED, D_MODEL), A_MERGED ** -0.5 * res_scale),
        'b_w_in': nrm(ks[5], (N_B, D_MODEL, B_IN_COLS), D_MODEL ** -0.5),
        'b_w_gate_up': nrm(ks[6], (N_B, B_GATE_RANK, B_KEY_DIM), B_GATE_RANK ** -0.5),
        'b_b_gate_up': nrm(ks[7], (N_B, B_KEY_DIM), 0.1),
        'b_g_norm': 1.0 + nrm(ks[8], (N_B, B_DV), 0.02),
        'b_w_o': nrm(ks[9], (N_B, B_VAL_DIM, D_MODEL), B_VAL_DIM ** -0.5 * res_scale),
        'f_w_in': nrm(ks[10], (DEPTH, D_MODEL, 2 * FFN_DIM), D_MODEL ** -0.5),
        'f_conv_w': nrm(ks[11], (DEPTH, CONV_WIDTH, FFN_DIM), CONV_WIDTH ** -0.5),
        'f_conv_b': nrm(ks[12], (DEPTH, FFN_DIM), 0.02),
        'f_w_down': nrm(ks[13], (DEPTH, FFN_DIM, D_MODEL), FFN_DIM ** -0.5 * res_scale),
        'norm_final': 1.0 + nrm(ks[14], (D_MODEL,), 0.02),
    }


def reference(x, positions, norm_mix, norm_ffn, a_w_qkv, a_w_o, b_w_in, b_w_gate_up,
              b_b_gate_up, b_g_norm, b_w_o, f_w_in, f_conv_w, f_conv_b, f_w_down, norm_final):
    cos, sin = rope_tables(positions, A_HEAD_DIM)
    for i in range(DEPTH):
        h = rms_norm(x, norm_mix[i])
        j = i // N_MIXERS
        if i % N_MIXERS == 0:
            x = x + mixer_dilated(h, a_w_qkv[j], a_w_o[j], cos, sin)
        else:
            x = x + mixer_gla(h, b_w_in[j], b_w_gate_up[j], b_b_gate_up[j], b_g_norm[j], b_w_o[j])
        h = rms_norm(x, norm_ffn[i])
        x = x + conv_ffn(h, f_w_in[i], f_conv_w[i], f_conv_b[i], f_w_down[i])
    return rms_norm(x, norm_final)
```

```python
import functools

import jax
import jax.numpy as jnp
from jax import lax
from jax.experimental import pallas as pl
from jax.experimental.pallas import tpu as pltpu

D_MODEL = 1024
RMS_EPS = 1e-6

A_GROUPS = ((128, 1), (512, 4), (2048, 16))
A_N_GROUPS = len(A_GROUPS)
A_HEADS_PER_GROUP = 8
A_HEAD_DIM = 64
A_GROUP_COLS = A_HEADS_PER_GROUP * A_HEAD_DIM
A_SPAN = 128
ROPE_THETA = 10000.0
NEG_INF = -1e30

B_HEADS = 4
B_KEY_DIM = 512
B_VAL_DIM = 1024
B_DK = B_KEY_DIM // B_HEADS
B_DV = B_VAL_DIM // B_HEADS
B_GATE_RANK = 16
B_GATE_NORMALIZER = 16.0
B_CHUNK = 64
B_SLAB = 256

FFN_DIM = 2816
FFN_CHUNK = 256
FFN_NCHUNK = FFN_DIM // FFN_CHUNK
CONV_WIDTH = 3

LANES = 128
VMEM_LIMIT = 52 * 1024 * 1024

BF16 = jnp.bfloat16
F32 = jnp.float32


def _resident(shape):
    nd = len(shape)
    return pl.BlockSpec(shape, lambda *_: (0,) * nd, pipeline_mode=pl.Buffered(1))


def _rms(x, g):
    ms = jnp.mean(x * x, axis=-1, keepdims=True)
    return x * lax.rsqrt(ms + RMS_EPS) * g


def _dot(a, b):
    return jnp.dot(a, b, preferred_element_type=F32)


def _dot_nt(a, b):
    return lax.dot_general(a, b, (((1,), (1,)), ((), ())), preferred_element_type=F32)


def _dot_tn(a, b):
    return lax.dot_general(a, b, (((0,), (0,)), ((), ())), preferred_element_type=F32)


def _rope_table_kernel(pos_ref, inv_ref, sign_ref, cos_ref, sin_ref):
    ang = pos_ref[...].astype(F32) * inv_ref[...]
    cos_ref[...] = jnp.cos(ang)
    sin_ref[...] = jnp.sin(ang) * sign_ref[...]


def _rope_tables(positions):
    t = positions.size
    tm = min(t, 2048)
    half = A_HEAD_DIM // 2
    inv = ROPE_THETA ** (-jnp.arange(0, A_HEAD_DIM, 2, dtype=F32) / A_HEAD_DIM)
    inv = jnp.tile(inv, LANES // half)[None, :]
    sign = jnp.where(jnp.arange(LANES) < LANES // 2, -1.0, 1.0).astype(F32)[None, :]
    return pl.pallas_call(
        _rope_table_kernel,
        grid=(t // tm,),
        in_specs=[
            pl.BlockSpec((tm, 1), lambda i: (i, 0)),
            pl.BlockSpec((1, LANES), lambda i: (0, 0)),
            pl.BlockSpec((1, LANES), lambda i: (0, 0)),
        ],
        out_specs=[pl.BlockSpec((tm, LANES), lambda i: (i, 0))] * 2,
        out_shape=[jax.ShapeDtypeStruct((t, LANES), F32)] * 2,
        name="rope_tables",
    )(positions.reshape(t, 1), inv, sign)


def _attn_proj_kernel(x_ref, g_ref, w_ref, cos_ref, sin_ref, *refs, tm):
    outs = refs[: 3 * A_N_GROUPS]
    scr = refs[3 * A_N_GROUPS]
    h = _rms(x_ref[0], g_ref[...]).astype(BF16)
    cos = cos_ref[...]
    sin = sin_ref[...]
    for part in range(3):
        for g, (_, dil) in enumerate(A_GROUPS):
            c0 = (part * A_N_GROUPS + g) * A_GROUP_COLS
            y = _dot(h, w_ref[:, c0:c0 + A_GROUP_COLS])
            o_ref = outs[g * 3 + part]
            for j in range(A_GROUP_COLS // LANES):
                lanes = slice(j * LANES, (j + 1) * LANES)
                yb = y[:, lanes]
                if part < 2:
                    yb = yb * cos + pltpu.roll(yb, LANES // 2, axis=1) * sin
                    if part == 0:
                        yb = yb * (A_HEAD_DIM ** -0.5)
                if dil == 1:
                    o_ref[0, 0, :, lanes] = yb.astype(BF16)
                else:
                    scr[j] = yb
                    for r in range(dil):
                        o_ref[0, r, :, lanes] = scr[j, pl.ds(r, tm // dil, stride=dil), :].astype(BF16)


def _attn_proj(x, g, w, cos, sin, tm=512):
    bsz, seq, _ = x.shape
    nt = seq // tm
    out_shapes, out_specs = [], []
    for _, dil in A_GROUPS:
        for _ in range(3):
            out_shapes.append(jax.ShapeDtypeStruct((bsz, dil, seq // dil, A_GROUP_COLS), BF16))
            out_specs.append(pl.BlockSpec((1, dil, tm // dil, A_GROUP_COLS), lambda b, i: (b, 0, i, 0)))
    return pl.pallas_call(
        functools.partial(_attn_proj_kernel, tm=tm),
        grid=(bsz, nt),
        in_specs=[
            pl.BlockSpec((1, tm, D_MODEL), lambda b, i: (b, i, 0)),
            _resident((1, D_MODEL)),
            _resident(w.shape),
            pl.BlockSpec((tm, LANES), lambda b, i: (b * nt + i, 0)),
            pl.BlockSpec((tm, LANES), lambda b, i: (b * nt + i, 0)),
        ],
        out_specs=out_specs,
        out_shape=out_shapes,
        scratch_shapes=[pltpu.VMEM((A_GROUP_COLS // LANES, tm, LANES), F32)],
        compiler_params=pltpu.CompilerParams(
            dimension_semantics=("parallel", "parallel"), vmem_limit_bytes=VMEM_LIMIT),
        name="attn_proj",
    )(x, g, w, cos, sin)


def _attn_core_kernel(*refs, seq):
    qkv = refs[: 3 * A_N_GROUPS]
    out_ref = refs[3 * A_N_GROUPS]
    o_scr, lse_scr = refs[3 * A_N_GROUPS + 1:]
    span = A_SPAN
    lane = lax.broadcasted_iota(jnp.int32, (1, LANES), 1)

    def block(g, dil, r, n, first):
        q_ref, k_ref, v_ref = qkv[3 * g: 3 * g + 3]
        qk_is_a = (lane % (LANES // 2)) < (LANES // 4)
        v_is_a = lane < (LANES // 2)
        nk = span if first else 2 * span
        row = lax.broadcasted_iota(jnp.int32, (span, nk), 0)
        col = lax.broadcasted_iota(jnp.int32, (span, nk), 1)
        if first:
            qb = q_ref[0, r, pl.ds(0, span), :]
            kb = k_ref[0, r, pl.ds(0, span), :]
            vb = v_ref[0, r, pl.ds(0, span), :]
            mask = col <= row
        else:
            k0 = pl.multiple_of((n - 1) * span, span)
            qb = q_ref[0, r, pl.ds(k0 + span, span), :]
            kb = k_ref[0, r, pl.ds(k0, 2 * span), :]
            vb = v_ref[0, r, pl.ds(k0, 2 * span), :]
            mask = (col >= row) & (col <= row + span)
        zero = jnp.zeros_like(qb)
        zero_v = jnp.zeros_like(vb)
        o_acc = None
        lse_acc = None
        for is_a in (True, False):
            qm = jnp.where(qk_is_a, qb, zero) if is_a else jnp.where(qk_is_a, zero, qb)
            vm = jnp.where(v_is_a, vb, zero_v) if is_a else jnp.where(v_is_a, zero_v, vb)
            s = jnp.where(mask, _dot_nt(qm, kb), NEG_INF)
            m = jnp.max(s, axis=-1, keepdims=True)
            p = jnp.exp(s - m)
            l = jnp.sum(p, axis=-1, keepdims=True)
            o = _dot(p.astype(BF16), vm) * (1.0 / l)
            lse = jnp.broadcast_to(m + jnp.log(l), (span, LANES))
            o_acc = o if o_acc is None else o_acc + o
            lse_acc = lse if lse_acc is None else jnp.where(v_is_a, lse_acc, lse)
        start = n * span * dil + r
        if dil == 1:
            rows = pl.ds(pl.multiple_of(start, span), span)
        else:
            rows = pl.ds(start, span, stride=dil)
        o_scr[g, rows, :] = o_acc
        lse_scr[g, rows, :] = lse_acc

    for g, (_, dil) in enumerate(A_GROUPS):
        nb = seq // dil // span
        for r in range(dil):
            block(g, dil, r, 0, True)
            if nb > 1:
                def body(n, carry, g=g, dil=dil, r=r):
                    block(g, dil, r, n, False)
                    return carry
                lax.fori_loop(1, nb, body, 0)

    lses = [lse_scr[g] for g in range(A_N_GROUPS)]
    mx = functools.reduce(jnp.maximum, lses)
    es = [jnp.exp(v - mx) for v in lses]
    inv = 1.0 / functools.reduce(lambda a, b: a + b, es)
    merged = functools.reduce(lambda a, b: a + b,
                              [(es[g] * inv) * o_scr[g] for g in range(A_N_GROUPS)])
    out_ref[0] = merged.astype(BF16)


def _attn_core(qkv, seq):
    bsz = qkv[0].shape[0]
    npair = A_GROUP_COLS // LANES
    in_specs = []
    for _, dil in A_GROUPS:
        for _ in range(3):
            in_specs.append(pl.BlockSpec((1, dil, seq // dil, LANES), lambda b, p: (b, 0, 0, p)))
    return pl.pallas_call(
        functools.partial(_attn_core_kernel, seq=seq),
        grid=(bsz, npair),
        in_specs=in_specs,
        out_specs=pl.BlockSpec((1, seq, LANES), lambda b, p: (b, 0, p)),
        out_shape=jax.ShapeDtypeStruct((bsz, seq, A_GROUP_COLS), BF16),
        scratch_shapes=[pltpu.VMEM((A_N_GROUPS, seq, LANES), F32),
                        pltpu.VMEM((A_N_GROUPS, seq, LANES), F32)],
        compiler_params=pltpu.CompilerParams(
            dimension_semantics=("parallel", "parallel"), vmem_limit_bytes=VMEM_LIMIT),
        name="attn_core",
    )(*qkv)


def _gla_proj_kernel(x_ref, g_ref, w_ref, wgd_ref, wgu_ref, bgu_ref,
                     q_ref, k_ref, v_ref, r_ref, glog_ref):
    h = _rms(x_ref[...], g_ref[...]).astype(BF16)
    q_ref[...] = (_dot(h, w_ref[:, :B_KEY_DIM]) * (B_DK ** -0.5)).astype(BF16)
    k_ref[...] = _dot(h, w_ref[:, B_KEY_DIM:2 * B_KEY_DIM]).astype(BF16)
    c0 = 2 * B_KEY_DIM
    v_ref[...] = _dot(h, w_ref[:, c0:c0 + B_VAL_DIM]).astype(BF16)
    r_ref[...] = _dot(h, w_ref[:, c0 + B_VAL_DIM:c0 + 2 * B_VAL_DIM])
    gd = _dot(h, wgd_ref[...]).astype(BF16)
    z = _dot(gd, wgu_ref[...]) + bgu_ref[...]
    log_sig = jnp.minimum(z, 0.0) - jnp.log(1.0 + jnp.exp(-jnp.abs(z)))
    glog_ref[...] = log_sig / B_GATE_NORMALIZER


def _gla_proj(x2, g, w_main, w_gd, w_gu, b_gu, tm=512):
    t = x2.shape[0]
    row = lambda i: (i, 0)
    return pl.pallas_call(
        _gla_proj_kernel,
        grid=(t // tm,),
        in_specs=[
            pl.BlockSpec((tm, D_MODEL), row),
            _resident((1, D_MODEL)),
            _resident(w_main.shape),
            _resident(w_gd.shape),
            _resident(w_gu.shape),
            _resident(b_gu.shape),
        ],
        out_specs=[
            pl.BlockSpec((tm, B_KEY_DIM), row),
            pl.BlockSpec((tm, B_KEY_DIM), row),
            pl.BlockSpec((tm, B_VAL_DIM), row),
            pl.BlockSpec((tm, B_VAL_DIM), row),
            pl.BlockSpec((tm, B_KEY_DIM), row),
        ],
        out_shape=[
            jax.ShapeDtypeStruct((t, B_KEY_DIM), BF16),
            jax.ShapeDtypeStruct((t, B_KEY_DIM), BF16),
            jax.ShapeDtypeStruct((t, B_VAL_DIM), BF16),
            jax.ShapeDtypeStruct((t, B_VAL_DIM), F32),
            jax.ShapeDtypeStruct((t, B_KEY_DIM), F32),
        ],
        compiler_params=pltpu.CompilerParams(
            dimension_semantics=("parallel",), vmem_limit_bytes=VMEM_LIMIT),
        name="gla_proj",
    )(x2, g, w_main, w_gd, w_gu, b_gu)


def _gla_core_kernel(q_ref, k_ref, v_ref, r_ref, glog_ref, gn_ref, tri_ref,
                     out_ref, b_scr, st_scr, *, seq):
    c = B_CHUNK
    tri = tri_ref[...]
    for s in range(seq // B_SLAB):
        rows = pl.ds(s * B_SLAB, B_SLAB)
        gl = glog_ref[0, rows, :]
        hi = gl.astype(BF16)
        rem = gl - hi.astype(F32)
        mid = rem.astype(BF16)
        lo = (rem - mid.astype(F32)).astype(BF16)
        b_scr[rows, :] = _dot(tri, hi) + _dot(tri, mid) + _dot(tri, lo)

    st_scr[...] = jnp.zeros_like(st_scr)
    ri = lax.broadcasted_iota(jnp.int32, (c, c), 0)
    ci = lax.broadcasted_iota(jnp.int32, (c, c), 1)
    causal = ci <= ri
    gn = gn_ref[...]

    def body(n, carry):
        base = pl.multiple_of(n * c, c)
        rows = pl.ds(base, c)
        bb = b_scr[rows, :]
        b_mid = b_scr[pl.ds(base + c // 2 - 1, 1), :]
        b_last = b_scr[pl.ds(base + c - 1, 1), :]
        qf = q_ref[0, rows, :].astype(F32)
        kf = k_ref[0, rows, :].astype(F32)
        vb = v_ref[0, rows, :]
        x_mid = bb - b_mid
        qe = (qf * jnp.exp(x_mid)).astype(BF16)
        ke = (kf * jnp.exp(-x_mid)).astype(BF16)
        a = jnp.where(causal, _dot_nt(qe, ke), 0.0)
        o = _dot(a.astype(BF16), vb)
        qi = (qf * jnp.exp(bb)).astype(BF16)
        ks = (kf * jnp.exp(b_last - bb)).astype(BF16)
        st = st_scr[...]
        o = o + _dot_nt(qi, st.astype(BF16))
        st_scr[...] = st * jnp.exp(b_last) + _dot_tn(vb, ks)
        o = _rms(o, gn)
        rr = r_ref[0, rows, :]
        out_ref[0, rows, :] = (o * (rr * (1.0 / (1.0 + jnp.exp(-rr))))).astype(BF16)
        return carry

    lax.fori_loop(0, seq // c, body, 0)


def _gla_core(q, k, v, r, glog, g_norm, seq):
    bsz = q.shape[0]
    idx = jnp.arange(B_SLAB)
    tri = ((idx[:, None] // B_CHUNK == idx[None, :] // B_CHUNK)
           & (idx[None, :] <= idx[:, None])).astype(BF16)
    hblock = lambda width: pl.BlockSpec((1, seq, width), lambda b, h: (b, 0, h))
    return pl.pallas_call(
        functools.partial(_gla_core_kernel, seq=seq),
        grid=(bsz, B_HEADS),
        in_specs=[hblock(B_DK), hblock(B_DK), hblock(B_DV), hblock(B_DV), hblock(B_DK),
                  _resident((1, B_DV)), _resident((B_SLAB, B_SLAB))],
        out_specs=hblock(B_DV),
        out_shape=jax.ShapeDtypeStruct((bsz, seq, B_VAL_DIM), BF16),
        scratch_shapes=[pltpu.VMEM((seq, B_DK), F32), pltpu.VMEM((B_DV, B_DK), F32)],
        compiler_params=pltpu.CompilerParams(
            dimension_semantics=("parallel", "parallel"), vmem_limit_bytes=VMEM_LIMIT),
        name="gla_core",
    )(q, k, v, r, glog, g_norm, tri)


def _ffn_kernel(x_ref, mix_ref, wo_ref, g_ref, win_ref, cw_ref, cb_ref, wd_ref, gf_ref,
                out_ref, carry_scr, act_scr, *, tm, final):
    i = pl.program_id(1)

    @pl.when(i == 0)
    def _():
        carry_scr[...] = jnp.zeros_like(carry_scr)

    x1 = x_ref[0] + _dot(mix_ref[0], wo_ref[...])
    h = _rms(x1, g_ref[...]).astype(BF16)
    row = lax.broadcasted_iota(jnp.int32, (tm, FFN_CHUNK), 0)
    for c in range(FFN_NCHUNK):
        cols = slice(c * FFN_CHUNK, (c + 1) * FFN_CHUNK)
        au = _dot(h, win_ref[:, 2 * c * FFN_CHUNK:2 * (c + 1) * FFN_CHUNK])
        a = au[:, :FFN_CHUNK]
        u = au[:, FFN_CHUNK:]
        prev2 = carry_scr[6:7, cols]
        prev1 = carry_scr[7:8, cols]
        a1 = jnp.where(row == 0, prev1, pltpu.roll(a, 1, axis=0))
        a2 = jnp.where(row == 0, prev2, jnp.where(row == 1, prev1, pltpu.roll(a, 2, axis=0)))
        carry_scr[:, cols] = a[tm - 8:, :]
        conv = a * cw_ref[2:3, cols] + a1 * cw_ref[1:2, cols] + a2 * cw_ref[0:1, cols] + cb_ref[:, cols]
        act = conv * (1.0 / (1.0 + jnp.exp(-conv))) * u
        act_scr[:, cols] = act.astype(BF16)
    y = x1 + _dot(act_scr[...], wd_ref[...])
    if final:
        y = _rms(y, gf_ref[...])
    out_ref[0] = y


def _ffn(x, mix, w_o, g, w_in, conv_w, conv_b, w_down, g_final, final, tm=512):
    bsz, seq, _ = x.shape
    kmix = mix.shape[-1]
    return pl.pallas_call(
        functools.partial(_ffn_kernel, tm=tm, final=final),
        grid=(bsz, seq // tm),
        in_specs=[
            pl.BlockSpec((1, tm, D_MODEL), lambda b, i: (b, i, 0)),
            pl.BlockSpec((1, tm, kmix), lambda b, i: (b, i, 0)),
            _resident(w_o.shape),
            _resident((1, D_MODEL)),
            _resident(w_in.shape),
            _resident(conv_w.shape),
            _resident(conv_b.shape),
            _resident(w_down.shape),
            _resident((1, D_MODEL)),
        ],
        out_specs=pl.BlockSpec((1, tm, D_MODEL), lambda b, i: (b, i, 0)),
        out_shape=jax.ShapeDtypeStruct(x.shape, F32),
        scratch_shapes=[pltpu.VMEM((8, FFN_DIM), F32), pltpu.VMEM((tm, FFN_DIM), BF16)],
        compiler_params=pltpu.CompilerParams(
            dimension_semantics=("arbitrary", "arbitrary"), vmem_limit_bytes=VMEM_LIMIT),
        name="ffn",
    )(x, mix, w_o, g, w_in, conv_w, conv_b, w_down, g_final)


def _permute_rope_cols(w):
    d, n = w.shape
    half = A_HEAD_DIM // 2
    w = w.reshape(d, n // LANES, 2, 2, half)
    return w.transpose(0, 1, 3, 2, 4).reshape(d, n)


def _prep_attn_w(w_qkv):
    nqk = 2 * A_N_GROUPS * A_GROUP_COLS
    w = jnp.concatenate([_permute_rope_cols(w_qkv[:, :nqk]), w_qkv[:, nqk:]], axis=1)
    return w.astype(BF16)


def _prep_ffn_w_in(w_in):
    d = w_in.shape[0]
    w = w_in.reshape(d, 2, FFN_NCHUNK, FFN_CHUNK).transpose(0, 2, 1, 3)
    return w.reshape(d, 2 * FFN_DIM).astype(BF16)


def kernel(x, positions, norm_mix, norm_ffn, a_w_qkv, a_w_o, b_w_in, b_w_gate_up,
           b_b_gate_up, b_g_norm, b_w_o, f_w_in, f_conv_w, f_conv_b, f_w_down, norm_final):
    bsz, seq, _ = x.shape
    depth = norm_mix.shape[0]
    cos, sin = _rope_tables(positions)
    g_final = norm_final[None, :]
    for i in range(depth):
        j = i // 2
        g_mix = norm_mix[i][None, :]
        if i % 2 == 0:
            qkv = _attn_proj(x, g_mix, _prep_attn_w(a_w_qkv[j]), cos, sin)
            mix = _attn_core(qkv, seq)
            w_o = a_w_o[j].astype(BF16)
        else:
            w = b_w_in[j]
            ncore = 2 * B_KEY_DIM + 2 * B_VAL_DIM
            w_gd = jnp.pad(w[:, ncore:], ((0, 0), (0, LANES - B_GATE_RANK))).astype(BF16)
            w_gu = jnp.pad(b_w_gate_up[j], ((0, LANES - B_GATE_RANK), (0, 0))).astype(BF16)
            q, k, v, r, glog = _gla_proj(
                x.reshape(bsz * seq, D_MODEL), g_mix, w[:, :ncore].astype(BF16),
                w_gd, w_gu, b_b_gate_up[j][None, :])
            shp = lambda t: t.reshape(bsz, seq, t.shape[-1])
            mix = _gla_core(shp(q), shp(k), shp(v), shp(r), shp(glog), b_g_norm[j][None, :], seq)
            w_o = b_w_o[j].astype(BF16)
        x = _ffn(x, mix, w_o, norm_ffn[i][None, :], _prep_ffn_w_in(f_w_in[i]), f_conv_w[i],
                 f_conv_b[i][None, :], f_w_down[i].astype(BF16), g_final, final=(i == depth - 1))
    return x
```

```python
import functools

import jax
import jax.numpy as jnp
from jax import lax
from jax.experimental import pallas as pl
from jax.experimental.pallas import tpu as pltpu

D_MODEL = 1024
RMS_EPS = 1e-6

A_GROUPS = ((128, 1), (512, 4), (2048, 16))
A_N_GROUPS = len(A_GROUPS)
A_HEADS_PER_GROUP = 8
A_HEAD_DIM = 64
A_GROUP_COLS = A_HEADS_PER_GROUP * A_HEAD_DIM
A_SPAN = 128
ROPE_THETA = 10000.0
NEG_INF = -1e30

B_HEADS = 4
B_KEY_DIM = 512
B_VAL_DIM = 1024
B_DK = B_KEY_DIM // B_HEADS
B_DV = B_VAL_DIM // B_HEADS
B_GATE_RANK = 16
B_GATE_NORMALIZER = 16.0
B_CHUNK = 64
B_SLAB = 256

FFN_DIM = 2816
FFN_CHUNK = 256
FFN_NCHUNK = FFN_DIM // FFN_CHUNK
CONV_WIDTH = 3

LANES = 128
VMEM_LIMIT = 52 * 1024 * 1024

BF16 = jnp.bfloat16
F32 = jnp.float32


def _resident(shape):
    nd = len(shape)
    return pl.BlockSpec(shape, lambda *_: (0,) * nd, pipeline_mode=pl.Buffered(1))


def _rms(x, g):
    ms = jnp.mean(x * x, axis=-1, keepdims=True)
    return x * lax.rsqrt(ms + RMS_EPS) * g


def _dot(a, b):
    return jnp.dot(a, b, preferred_element_type=F32)


def _dot_nt(a, b):
    return lax.dot_general(a, b, (((1,), (1,)), ((), ())), preferred_element_type=F32)


def _dot_tn(a, b):
    return lax.dot_general(a, b, (((0,), (0,)), ((), ())), preferred_element_type=F32)


def _rope_table_kernel(pos_ref, inv_ref, sign_ref, cos_ref, sin_ref):
    ang = pos_ref[...].astype(F32) * inv_ref[...]
    cos_ref[...] = jnp.cos(ang)
    sin_ref[...] = jnp.sin(ang) * sign_ref[...]


def _rope_tables(positions):
    t = positions.size
    tm = min(t, 2048)
    half = A_HEAD_DIM // 2
    inv = ROPE_THETA ** (-jnp.arange(0, A_HEAD_DIM, 2, dtype=F32) / A_HEAD_DIM)
    inv = jnp.tile(inv, LANES // half)[None, :]
    sign = jnp.where(jnp.arange(LANES) < LANES // 2, -1.0, 1.0).astype(F32)[None, :]
    return pl.pallas_call(
        _rope_table_kernel,
        grid=(t // tm,),
        in_specs=[
            pl.BlockSpec((tm, 1), lambda i: (i, 0)),
            pl.BlockSpec((1, LANES), lambda i: (0, 0)),
            pl.BlockSpec((1, LANES), lambda i: (0, 0)),
        ],
        out_specs=[pl.BlockSpec((tm, LANES), lambda i: (i, 0))] * 2,
        out_shape=[jax.ShapeDtypeStruct((t, LANES), F32)] * 2,
        name="rope_tables",
    )(positions.reshape(t, 1), inv, sign)


def _attn_proj_kernel(x_ref, g_ref, w_ref, cos_ref, sin_ref, *refs, tm):
    outs = refs[: 3 * A_N_GROUPS]
    scr = refs[3 * A_N_GROUPS]
    h = _rms(x_ref[0], g_ref[...]).astype(BF16)
    cos = cos_ref[...]
    sin = sin_ref[...]
    for part in range(3):
        for g, (_, dil) in enumerate(A_GROUPS):
            c0 = (part * A_N_GROUPS + g) * A_GROUP_COLS
            y = _dot(h, w_ref[:, c0:c0 + A_GROUP_COLS])
            o_ref = outs[g * 3 + part]
            for j in range(A_GROUP_COLS // LANES):
                lanes = slice(j * LANES, (j + 1) * LANES)
                yb = y[:, lanes]
                if part < 2:
                    yb = yb * cos + pltpu.roll(yb, LANES // 2, axis=1) * sin
                    if part == 0:
                        yb = yb * (A_HEAD_DIM ** -0.5)
                if dil == 1:
                    o_ref[0, 0, :, lanes] = yb.astype(BF16)
                else:
                    scr[j] = yb
                    for r in range(dil):
                        o_ref[0, r, :, lanes] = scr[j, pl.ds(r, tm // dil, stride=dil), :].astype(BF16)


def _attn_proj(x, g, w, cos, sin, tm=512):
    bsz, seq, _ = x.shape
    nt = seq // tm
    out_shapes, out_specs = [], []
    for _, dil in A_GROUPS:
        for _ in range(3):
            out_shapes.append(jax.ShapeDtypeStruct((bsz, dil, seq // dil, A_GROUP_COLS), BF16))
            out_specs.append(pl.BlockSpec((1, dil, tm // dil, A_GROUP_COLS), lambda b, i: (b, 0, i, 0)))
    return pl.pallas_call(
        functools.partial(_attn_proj_kernel, tm=tm),
        grid=(bsz, nt),
        in_specs=[
            pl.BlockSpec((1, tm, D_MODEL), lambda b, i: (b, i, 0)),
            _resident((1, D_MODEL)),
            _resident(w.shape),
            pl.BlockSpec((tm, LANES), lambda b, i: (b * nt + i, 0)),
            pl.BlockSpec((tm, LANES), lambda b, i: (b * nt + i, 0)),
        ],
        out_specs=out_specs,
        out_shape=out_shapes,
        scratch_shapes=[pltpu.VMEM((A_GROUP_COLS // LANES, tm, LANES), F32)],
        compiler_params=pltpu.CompilerParams(
            dimension_semantics=("parallel", "parallel"), vmem_limit_bytes=VMEM_LIMIT),
        name="attn_proj",
    )(x, g, w, cos, sin)


def _attn_core_kernel(*refs, seq):
    qkv = refs[: 3 * A_N_GROUPS]
    out_ref = refs[3 * A_N_GROUPS]
    o_scr, lse_scr, band_scr, causal_scr = refs[3 * A_N_GROUPS + 1:]
    span = A_SPAN
    lane = lax.broadcasted_iota(jnp.int32, (1, LANES), 1)

    qk_is_a = (lane % (LANES // 2)) < (LANES // 4)
    v_is_a = lane < (LANES // 2)

    row = lax.broadcasted_iota(jnp.int32, (2 * span, 2 * span), 0) % span
    col = lax.broadcasted_iota(jnp.int32, (2 * span, 2 * span), 1)
    band_scr[...] = jnp.where((col >= row) & (col <= row + span), 0.0, NEG_INF)
    row1 = lax.broadcasted_iota(jnp.int32, (2 * span, span), 0) % span
    col1 = lax.broadcasted_iota(jnp.int32, (2 * span, span), 1)
    causal_scr[...] = jnp.where(col1 <= row1, 0.0, NEG_INF)

    def block(g, dil, r, n, first):
        q_ref, k_ref, v_ref = qkv[3 * g: 3 * g + 3]
        if first:
            qb = q_ref[0, r, pl.ds(0, span), :]
            kb = k_ref[0, r, pl.ds(0, span), :]
            vb = v_ref[0, r, pl.ds(0, span), :]
            bias = causal_scr[...]
        else:
            k0 = pl.multiple_of((n - 1) * span, span)
            qb = q_ref[0, r, pl.ds(k0 + span, span), :]
            kb = k_ref[0, r, pl.ds(k0, 2 * span), :]
            vb = v_ref[0, r, pl.ds(k0, 2 * span), :]
            bias = band_scr[...]
        zero = jnp.zeros_like(qb)
        q2 = jnp.concatenate([jnp.where(qk_is_a, qb, zero), jnp.where(qk_is_a, zero, qb)], axis=0)
        s = _dot_nt(q2, kb) + bias
        m_a = jnp.max(s[:span], axis=-1, keepdims=True)
        m_b = jnp.max(s[span:], axis=-1, keepdims=True)
        p = jnp.concatenate([jnp.exp(s[:span] - m_a), jnp.exp(s[span:] - m_b)], axis=0).astype(BF16)
        res = _dot(p, jnp.concatenate([vb, jnp.ones_like(vb)], axis=1))
        l_a = res[:span, LANES:]
        l_b = res[span:, LANES:]
        o = jnp.where(v_is_a, res[:span, :LANES] * (1.0 / l_a), res[span:, :LANES] * (1.0 / l_b))
        lse = jnp.where(v_is_a, m_a + jnp.log(l_a), m_b + jnp.log(l_b))
        start = n * span * dil + r
        if dil == 1:
            rows = pl.ds(pl.multiple_of(start, span), span)
        else:
            rows = pl.ds(start, span, stride=dil)
        o_scr[g, rows, :] = o
        lse_scr[g, rows, :] = lse

    unroll = 4
    for g, (_, dil) in enumerate(A_GROUPS):
        nb = seq // dil // span
        if dil == 1:
            block(g, dil, 0, 0, True)
            per = 3
            def body(i, carry, g=g, dil=dil, per=per):
                for j in range(per):
                    block(g, dil, 0, 1 + i * per + j, False)
                return carry
            lax.fori_loop(0, (nb - 1) // per, body, 0)
        elif nb > 1:
            def body(r, carry, g=g, dil=dil, nb=nb):
                block(g, dil, r, 0, True)
                for n in range(1, nb):
                    block(g, dil, r, n, False)
                return carry
            lax.fori_loop(0, dil, body, 0)
        else:
            def body(i, carry, g=g, dil=dil):
                for j in range(unroll):
                    block(g, dil, i * unroll + j, 0, True)
                return carry
            lax.fori_loop(0, dil // unroll, body, 0)

    lses = [lse_scr[g] for g in range(A_N_GROUPS)]
    mx = functools.reduce(jnp.maximum, lses)
    es = [jnp.exp(v - mx) for v in lses]
    inv = 1.0 / functools.reduce(lambda a, b: a + b, es)
    merged = functools.reduce(lambda a, b: a + b,
                              [(es[g] * inv) * o_scr[g] for g in range(A_N_GROUPS)])
    out_ref[0] = merged.astype(BF16)


def _attn_core(qkv, seq):
    bsz = qkv[0].shape[0]
    npair = A_GROUP_COLS // LANES
    in_specs = []
    for _, dil in A_GROUPS:
        for _ in range(3):
            in_specs.append(pl.BlockSpec((1, dil, seq // dil, LANES), lambda b, p: (b, 0, 0, p)))
    return pl.pallas_call(
        functools.partial(_attn_core_kernel, seq=seq),
        grid=(bsz, npair),
        in_specs=in_specs,
        out_specs=pl.BlockSpec((1, seq, LANES), lambda b, p: (b, 0, p)),
        out_shape=jax.ShapeDtypeStruct((bsz, seq, A_GROUP_COLS), BF16),
        scratch_shapes=[pltpu.VMEM((A_N_GROUPS, seq, LANES), F32),
                        pltpu.VMEM((A_N_GROUPS, seq, LANES), F32),
                        pltpu.VMEM((2 * A_SPAN, 2 * A_SPAN), F32),
                        pltpu.VMEM((2 * A_SPAN, A_SPAN), F32)],
        compiler_params=pltpu.CompilerParams(
            dimension_semantics=("parallel", "parallel"), vmem_limit_bytes=VMEM_LIMIT),
        name="attn_core",
    )(*qkv)


def _gla_proj_kernel(x_ref, g_ref, w_ref, wgd_ref, wgu_ref, bgu_ref, tri_ref,
                     q_ref, k_ref, v_ref, r_ref, b_ref, *, tm):
    h = _rms(x_ref[...], g_ref[...]).astype(BF16)
    q_ref[...] = (_dot(h, w_ref[:, :B_KEY_DIM]) * (B_DK ** -0.5)).astype(BF16)
    k_ref[...] = _dot(h, w_ref[:, B_KEY_DIM:2 * B_KEY_DIM]).astype(BF16)
    c0 = 2 * B_KEY_DIM
    v_ref[...] = _dot(h, w_ref[:, c0:c0 + B_VAL_DIM]).astype(BF16)
    r_ref[...] = _dot(h, w_ref[:, c0 + B_VAL_DIM:c0 + 2 * B_VAL_DIM])
    gd = _dot(h, wgd_ref[...]).astype(BF16)
    z = _dot(gd, wgu_ref[...]) + bgu_ref[...]
    log_sig = jnp.minimum(z, 0.0) - jnp.log(1.0 + jnp.exp(-jnp.abs(z)))
    glog = log_sig / B_GATE_NORMALIZER
    hi = glog.astype(BF16)
    rem = glog - hi.astype(F32)
    mid = rem.astype(BF16)
    lo = (rem - mid.astype(F32)).astype(BF16)
    tri = tri_ref[...]
    for s in range(tm // B_SLAB):
        rows = slice(s * B_SLAB, (s + 1) * B_SLAB)
        b_ref[rows, :] = _dot(tri, hi[rows]) + _dot(tri, mid[rows]) + _dot(tri, lo[rows])


def _gla_proj(x2, g, w_main, w_gd, w_gu, b_gu, tm=512):
    t = x2.shape[0]
    row = lambda i: (i, 0)
    idx = jnp.arange(B_SLAB)
    tri = ((idx[:, None] // B_CHUNK == idx[None, :] // B_CHUNK)
           & (idx[None, :] <= idx[:, None])).astype(BF16)
    return pl.pallas_call(
        functools.partial(_gla_proj_kernel, tm=tm),
        grid=(t // tm,),
        in_specs=[
            pl.BlockSpec((tm, D_MODEL), row),
            _resident((1, D_MODEL)),
            _resident(w_main.shape),
            _resident(w_gd.shape),
            _resident(w_gu.shape),
            _resident(b_gu.shape),
            _resident(tri.shape),
        ],
        out_specs=[
            pl.BlockSpec((tm, B_KEY_DIM), row),
            pl.BlockSpec((tm, B_KEY_DIM), row),
            pl.BlockSpec((tm, B_VAL_DIM), row),
            pl.BlockSpec((tm, B_VAL_DIM), row),
            pl.BlockSpec((tm, B_KEY_DIM), row),
        ],
        out_shape=[
            jax.ShapeDtypeStruct((t, B_KEY_DIM), BF16),
            jax.ShapeDtypeStruct((t, B_KEY_DIM), BF16),
            jax.ShapeDtypeStruct((t, B_VAL_DIM), BF16),
            jax.ShapeDtypeStruct((t, B_VAL_DIM), F32),
            jax.ShapeDtypeStruct((t, B_KEY_DIM), F32),
        ],
        compiler_params=pltpu.CompilerParams(
            dimension_semantics=("parallel",), vmem_limit_bytes=VMEM_LIMIT),
        name="gla_proj",
    )(x2, g, w_main, w_gd, w_gu, b_gu, tri)


def _gla_core_kernel(q_ref, k_ref, v_ref, r_ref, b_ref, gn_ref, out_ref, *, seq, unroll):
    c = B_CHUNK
    ri = lax.broadcasted_iota(jnp.int32, (c, c), 0)
    ci = lax.broadcasted_iota(jnp.int32, (c, c), 1)
    gn = gn_ref[...]

    def chunk(n, st):
        base = pl.multiple_of(n * c, c)
        rows = pl.ds(base, c)
        bb = b_ref[0, rows, :]
        b_mid = b_ref[0, pl.ds(base + c // 2 - 1, 1), :]
        b_last = b_ref[0, pl.ds(base + c - 1, 1), :]
        qf = q_ref[0, rows, :].astype(F32)
        kf = k_ref[0, rows, :].astype(F32)
        vb = v_ref[0, rows, :]
        x_mid = bb - b_mid
        qe = (qf * jnp.exp(x_mid)).astype(BF16)
        ke = (kf * jnp.exp(-x_mid)).astype(BF16)
        a = jnp.where(ci <= ri, _dot_nt(qe, ke), 0.0)
        qi = (qf * jnp.exp(bb)).astype(BF16)
        ks = (kf * jnp.exp(b_last - bb)).astype(BF16)
        o = _dot(a.astype(BF16), vb) + _dot_nt(qi, st.astype(BF16))
        st = st * jnp.exp(b_last) + _dot_tn(vb, ks)
        o = _rms(o, gn)
        rr = r_ref[0, rows, :]
        out_ref[0, rows, :] = (o * (rr * (1.0 / (1.0 + jnp.exp(-rr))))).astype(BF16)
        return st

    def body(i, st):
        for j in range(unroll):
            st = chunk(i * unroll + j, st)
        return st

    lax.fori_loop(0, seq // c // unroll, body, jnp.zeros((B_DV, B_DK), F32))


def _gla_core(q, k, v, r, b, g_norm, seq, unroll=8):
    bsz = q.shape[0]
    hblock = lambda width: pl.BlockSpec((1, seq, width), lambda b, h: (b, 0, h))
    return pl.pallas_call(
        functools.partial(_gla_core_kernel, seq=seq, unroll=unroll),
        grid=(bsz, B_HEADS),
        in_specs=[hblock(B_DK), hblock(B_DK), hblock(B_DV), hblock(B_DV), hblock(B_DK),
                  _resident((1, B_DV))],
        out_specs=hblock(B_DV),
        out_shape=jax.ShapeDtypeStruct((bsz, seq, B_VAL_DIM), BF16),
        compiler_params=pltpu.CompilerParams(
            dimension_semantics=("parallel", "parallel"), vmem_limit_bytes=VMEM_LIMIT),
        name="gla_core",
    )(q, k, v, r, b, g_norm)


def _ffn_kernel(x_ref, mix_ref, wo_ref, g_ref, win_ref, cw_ref, cb_ref, wd_ref, gf_ref,
                out_ref, carry_scr, act_scr, *, tm, final):
    i = pl.program_id(1)

    @pl.when(i == 0)
    def _():
        carry_scr[...] = jnp.zeros_like(carry_scr)

    x1 = x_ref[0] + _dot(mix_ref[0], wo_ref[...])
    h = _rms(x1, g_ref[...]).astype(BF16)
    row = lax.broadcasted_iota(jnp.int32, (tm, FFN_CHUNK), 0)
    for c in range(FFN_NCHUNK):
        cols = slice(c * FFN_CHUNK, (c + 1) * FFN_CHUNK)
        au = _dot(h, win_ref[:, 2 * c * FFN_CHUNK:2 * (c + 1) * FFN_CHUNK])
        a = au[:, :FFN_CHUNK]
        u = au[:, FFN_CHUNK:]
        prev2 = carry_scr[6:7, cols]
        prev1 = carry_scr[7:8, cols]
        a1 = jnp.where(row == 0, prev1, pltpu.roll(a, 1, axis=0))
        a2 = jnp.where(row == 0, prev2, jnp.where(row == 1, prev1, pltpu.roll(a, 2, axis=0)))
        carry_scr[:, cols] = a[tm - 8:, :]
        conv = a * cw_ref[2:3, cols] + a1 * cw_ref[1:2, cols] + a2 * cw_ref[0:1, cols] + cb_ref[:, cols]
        act = conv * (1.0 / (1.0 + jnp.exp(-conv))) * u
        act_scr[:, cols] = act.astype(BF16)
    y = x1 + _dot(act_scr[...], wd_ref[...])
    if final:
        y = _rms(y, gf_ref[...])
    out_ref[0] = y


def _ffn(x, mix, w_o, g, w_in, conv_w, conv_b, w_down, g_final, final, tm=512):
    bsz, seq, _ = x.shape
    kmix = mix.shape[-1]
    return pl.pallas_call(
        functools.partial(_ffn_kernel, tm=tm, final=final),
        grid=(bsz, seq // tm),
        in_specs=[
            pl.BlockSpec((1, tm, D_MODEL), lambda b, i: (b, i, 0)),
            pl.BlockSpec((1, tm, kmix), lambda b, i: (b, i, 0)),
            _resident(w_o.shape),
            _resident((1, D_MODEL)),
            _resident(w_in.shape),
            _resident(conv_w.shape),
            _resident(conv_b.shape),
            _resident(w_down.shape),
            _resident((1, D_MODEL)),
        ],
        out_specs=pl.BlockSpec((1, tm, D_MODEL), lambda b, i: (b, i, 0)),
        out_shape=jax.ShapeDtypeStruct(x.shape, F32),
        scratch_shapes=[pltpu.VMEM((8, FFN_DIM), F32), pltpu.VMEM((tm, FFN_DIM), BF16)],
        compiler_params=pltpu.CompilerParams(
            dimension_semantics=("arbitrary", "arbitrary"), vmem_limit_bytes=VMEM_LIMIT),
        name="ffn",
    )(x, mix, w_o, g, w_in, conv_w, conv_b, w_down, g_final)


def _permute_rope_cols(w):
    d, n = w.shape
    half = A_HEAD_DIM // 2
    w = w.reshape(d, n // LANES, 2, 2, half)
    return w.transpose(0, 1, 3, 2, 4).reshape(d, n)


def _prep_attn_w(w_qkv):
    nqk = 2 * A_N_GROUPS * A_GROUP_COLS
    w = jnp.concatenate([_permute_rope_cols(w_qkv[:, :nqk]), w_qkv[:, nqk:]], axis=1)
    return w.astype(BF16)


def _prep_ffn_w_in(w_in):
    d = w_in.shape[0]
    w = w_in.reshape(d, 2, FFN_NCHUNK, FFN_CHUNK).transpose(0, 2, 1, 3)
    return w.reshape(d, 2 * FFN_DIM).astype(BF16)


def kernel(x, positions, norm_mix, norm_ffn, a_w_qkv, a_w_o, b_w_in, b_w_gate_up,
           b_b_gate_up, b_g_norm, b_w_o, f_w_in, f_conv_w, f_conv_b, f_w_down, norm_final):
    bsz, seq, _ = x.shape
    depth = norm_mix.shape[0]
    cos, sin = _rope_tables(positions)
    g_final = norm_final[None, :]
    for i in range(depth):
        j = i // 2
        g_mix = norm_mix[i][None, :]
        if i % 2 == 0:
            qkv = _attn_proj(x, g_mix, _prep_attn_w(a_w_qkv[j]), cos, sin)
            mix = _attn_core(qkv, seq)
            w_o = a_w_o[j].astype(BF16)
        else:
            w = b_w_in[j]
            ncore = 2 * B_KEY_DIM + 2 * B_VAL_DIM
            w_gd = jnp.pad(w[:, ncore:], ((0, 0), (0, LANES - B_GATE_RANK))).astype(BF16)
            w_gu = jnp.pad(b_w_gate_up[j], ((0, LANES - B_GATE_RANK), (0, 0))).astype(BF16)
            q, k, v, r, glog = _gla_proj(
                x.reshape(bsz * seq, D_MODEL), g_mix, w[:, :ncore].astype(BF16),
                w_gd, w_gu, b_b_gate_up[j][None, :])
            shp = lambda t: t.reshape(bsz, seq, t.shape[-1])
            mix = _gla_core(shp(q), shp(k), shp(v), shp(r), shp(glog), b_g_norm[j][None, :], seq)
            w_o = b_w_o[j].astype(BF16)
        x = _ffn(x, mix, w_o, norm_ffn[i][None, :], _prep_ffn_w_in(f_w_in[i]), f_conv_w[i],
                 f_conv_b[i][None, :], f_w_down[i].astype(BF16), g_final, final=(i == depth - 1))
    return x
```

```python
import functools

import jax
import jax.numpy as jnp
from jax import lax
from jax.experimental import pallas as pl
from jax.experimental.pallas import tpu as pltpu

D_MODEL = 1024
RMS_EPS = 1e-6

A_GROUPS = ((128, 1), (512, 4), (2048, 16))
A_N_GROUPS = len(A_GROUPS)
A_HEADS_PER_GROUP = 8
A_HEAD_DIM = 64
A_GROUP_COLS = A_HEADS_PER_GROUP * A_HEAD_DIM
A_SPAN = 128
A_BLOCKS_PER_BODY = (15, 16, 16)
A_SCORE_LOOKAHEAD = 3
ROPE_THETA = 10000.0
NEG_INF = -1e30
LOG2_E = 1.4426950408889634

B_HEADS = 4
B_KEY_DIM = 512
B_VAL_DIM = 1024
B_DK = B_KEY_DIM // B_HEADS
B_DV = B_VAL_DIM // B_HEADS
B_GATE_RANK = 16
B_GATE_NORMALIZER = 16.0
B_CHUNK = 64
B_SLAB = 256

FFN_DIM = 2816
FFN_CHUNK = 256
FFN_NCHUNK = FFN_DIM // FFN_CHUNK
CONV_WIDTH = 3

LANES = 128
VMEM_LIMIT = 52 * 1024 * 1024

BF16 = jnp.bfloat16
F32 = jnp.float32


def _resident(shape, layer=None):
    nd = len(shape)
    if layer is None:
        return pl.BlockSpec(shape, lambda *_: (0,) * nd, pipeline_mode=pl.Buffered(1))
    return pl.BlockSpec((None,) + tuple(shape), lambda *_: (layer,) + (0,) * nd,
                        pipeline_mode=pl.Buffered(1))


def _rms(x, g):
    ms = jnp.mean(x * x, axis=-1, keepdims=True)
    return x * lax.rsqrt(ms + RMS_EPS) * g


def _dot(a, b):
    return jnp.dot(a, b, preferred_element_type=F32)


def _dot_nt(a, b):
    return lax.dot_general(a, b, (((1,), (1,)), ((), ())), preferred_element_type=F32)


def _dot_tn(a, b):
    return lax.dot_general(a, b, (((0,), (0,)), ((), ())), preferred_element_type=F32)


def _rope_table_kernel(pos_ref, inv_ref, cos_ref, sin_lo_ref, sin_hi_ref):
    ang = pos_ref[...].astype(F32) * inv_ref[...]
    cos_ref[...] = jnp.cos(ang)
    sin = jnp.sin(ang)
    lane = lax.broadcasted_iota(jnp.int32, sin.shape, 1)
    is_lo = (lane % A_HEAD_DIM) < (A_HEAD_DIM // 2)
    sin_lo_ref[...] = jnp.where(is_lo, -sin, 0.0)
    sin_hi_ref[...] = jnp.where(is_lo, 0.0, sin)


def _rope_tables(positions):
    t = positions.size
    tm = min(t, 2048)
    half = A_HEAD_DIM // 2
    inv = ROPE_THETA ** (-jnp.arange(0, A_HEAD_DIM, 2, dtype=F32) / A_HEAD_DIM)
    inv = jnp.tile(inv, LANES // half)[None, :]
    return pl.pallas_call(
        _rope_table_kernel,
        grid=(t // tm,),
        in_specs=[
            pl.BlockSpec((tm, 1), lambda i: (i, 0)),
            pl.BlockSpec((1, LANES), lambda i: (0, 0)),
        ],
        out_specs=[pl.BlockSpec((tm, LANES), lambda i: (i, 0))] * 3,
        out_shape=[jax.ShapeDtypeStruct((t, LANES), F32)] * 3,
        name="rope_tables",
    )(positions.reshape(t, 1), inv)


def _attn_proj_kernel(x_ref, g_ref, w_ref, cos_ref, sin_lo_ref, sin_hi_ref, *refs, tm):
    outs = refs[: 3 * A_N_GROUPS]
    scr = refs[3 * A_N_GROUPS]
    h = _rms(x_ref[0], g_ref[...]).astype(BF16)
    cos = cos_ref[...]
    sin_lo = sin_lo_ref[...]
    sin_hi = sin_hi_ref[...]
    half = A_HEAD_DIM // 2

    def project(part, g):
        c0 = (part * A_N_GROUPS + g) * A_GROUP_COLS
        return _dot(h, w_ref[:, c0:c0 + A_GROUP_COLS])

    def emit(part, g, y, slot):
        dil = A_GROUPS[g][1]
        o_ref = outs[g * 3 + part]
        for j in range(A_GROUP_COLS // LANES):
            lanes = slice(j * LANES, (j + 1) * LANES)
            yb = y[:, lanes]
            if part < 2:
                yb = (yb * cos + pltpu.roll(yb, LANES - half, axis=1) * sin_lo
                      + pltpu.roll(yb, half, axis=1) * sin_hi)
                if part == 0:
                    yb = yb * (A_HEAD_DIM ** -0.5 * LOG2_E)
            if dil == 1:
                o_ref[0, 0, :, lanes] = yb.astype(BF16)
            else:
                scr[slot, j] = yb
                for r in range(dil):
                    o_ref[0, r, :, lanes] = scr[slot, j, pl.ds(r, tm // dil, stride=dil), :].astype(BF16)

    order = [(0, 2), (1, 2), (0, 1), (1, 1), (2, 2), (2, 1), (0, 0), (1, 0), (2, 0)]
    y_next = project(*order[0])
    for idx, (part, g) in enumerate(order):
        y = y_next
        if idx + 1 < len(order):
            y_next = project(*order[idx + 1])
        emit(part, g, y, idx % 2)


def _attn_proj(x, g, w, layer, tables, tm=1024):
    bsz, seq, _ = x.shape
    nt = seq // tm
    out_shapes, out_specs = [], []
    for _, dil in A_GROUPS:
        for _ in range(3):
            out_shapes.append(jax.ShapeDtypeStruct((bsz, dil, seq // dil, A_GROUP_COLS), BF16))
            out_specs.append(pl.BlockSpec((1, dil, tm // dil, A_GROUP_COLS), lambda b, i: (b, 0, i, 0)))
    return pl.pallas_call(
        functools.partial(_attn_proj_kernel, tm=tm),
        grid=(bsz, nt),
        in_specs=[
            pl.BlockSpec((1, tm, D_MODEL), lambda b, i: (b, i, 0)),
            _resident((1, D_MODEL), layer[0]),
            _resident(w.shape[1:], layer[1]),
        ] + [pl.BlockSpec((tm, LANES), lambda b, i: (b * nt + i, 0))] * len(tables),
        out_specs=out_specs,
        out_shape=out_shapes,
        scratch_shapes=[pltpu.VMEM((2, A_GROUP_COLS // LANES, tm, LANES), F32)],
        compiler_params=pltpu.CompilerParams(
            dimension_semantics=("parallel", "parallel"), vmem_limit_bytes=VMEM_LIMIT),
        name="attn_proj",
    )(x, g, w, *tables)


def _attn_core_kernel(*refs, seq):
    qkv = refs[: 3 * A_N_GROUPS]
    out_ref = refs[3 * A_N_GROUPS]
    o_scr, lse_scr, band_scr, causal_scr = refs[3 * A_N_GROUPS + 1:]
    span = A_SPAN
    lane = lax.broadcasted_iota(jnp.int32, (1, LANES), 1)
    v_is_a = lane < (LANES // 2)
    qk_is_a = v_is_a

    row = lax.broadcasted_iota(jnp.int32, (2 * span, 2 * span), 0) % span
    col = lax.broadcasted_iota(jnp.int32, (2 * span, 2 * span), 1)
    band_scr[...] = jnp.where((col >= row) & (col <= row + span), 0.0, NEG_INF)
    row1 = lax.broadcasted_iota(jnp.int32, (2 * span, span), 0) % span
    col1 = lax.broadcasted_iota(jnp.int32, (2 * span, span), 1)
    causal_scr[...] = jnp.where(col1 <= row1, 0.0, NEG_INF)

    def scores(g, r, n, first):
        q_ref, k_ref, _ = qkv[3 * g: 3 * g + 3]
        if first:
            qb = q_ref[0, r, pl.ds(0, span), :]
            kb = k_ref[0, r, pl.ds(0, span), :]
            bias = causal_scr[...]
        else:
            k0 = pl.multiple_of((n - 1) * span, span)
            qb = q_ref[0, r, pl.ds(k0 + span, span), :]
            kb = k_ref[0, r, pl.ds(k0, 2 * span), :]
            bias = band_scr[...]
        zero = jnp.zeros_like(qb)
        q2 = jnp.concatenate([jnp.where(qk_is_a, qb, zero), jnp.where(qk_is_a, zero, qb)], axis=0)
        return _dot_nt(q2, kb) + bias

    def finish(g, dil, r, n, first, s):
        v_ref = qkv[3 * g + 2]
        if first:
            vb = v_ref[0, r, pl.ds(0, span), :]
        else:
            vb = v_ref[0, r, pl.ds(pl.multiple_of((n - 1) * span, span), 2 * span), :]
        m_a = jnp.max(s[:span], axis=-1, keepdims=True)
        m_b = jnp.max(s[span:], axis=-1, keepdims=True)
        p = jnp.concatenate([jnp.exp2(s[:span] - m_a), jnp.exp2(s[span:] - m_b)], axis=1).astype(BF16)
        zero = jnp.zeros_like(vb)
        ind_a = jnp.broadcast_to(jnp.where(v_is_a, 1.0, 0.0), vb.shape).astype(BF16)
        ind_b = jnp.broadcast_to(jnp.where(v_is_a, 0.0, 1.0), vb.shape).astype(BF16)
        rhs = jnp.concatenate([
            jnp.concatenate([jnp.where(v_is_a, vb, zero), ind_a], axis=1),
            jnp.concatenate([jnp.where(v_is_a, zero, vb), ind_b], axis=1),
        ], axis=0)
        res = _dot(p, rhs)
        l = res[:, LANES:]
        o = res[:, :LANES] * (1.0 / l)
        lse = jnp.where(v_is_a, m_a, m_b) + jnp.log2(l)
        start = n * span * dil + r
        if dil == 1:
            rows = pl.ds(pl.multiple_of(start, span), span)
        else:
            rows = pl.ds(start, span, stride=dil)
        o_scr[g, rows, :] = o
        lse_scr[g, rows, :] = lse

    def blocks(g, dil, todo):
        pending = []
        for blk in todo:
            pending.append((blk, scores(g, *blk)))
            if len(pending) > A_SCORE_LOOKAHEAD:
                (r, n, first), s = pending.pop(0)
                finish(g, dil, r, n, first, s)
        for (r, n, first), s in pending:
            finish(g, dil, r, n, first, s)

    for g, (_, dil) in enumerate(A_GROUPS):
        nb = seq // dil // span
        if dil == 1:
            blocks(g, dil, [(0, 0, True)])
            per = A_BLOCKS_PER_BODY[g]
            def body(i, carry, g=g, dil=dil, per=per):
                blocks(g, dil, [(0, 1 + i * per + j, False) for j in range(per)])
                return carry
            lax.fori_loop(0, (nb - 1) // per, body, 0)
        else:
            streams = A_BLOCKS_PER_BODY[g] // nb
            def body(i, carry, g=g, dil=dil, nb=nb, streams=streams):
                blocks(g, dil, [(i * streams + j, n, n == 0)
                                for j in range(streams) for n in range(nb)])
                return carry
            lax.fori_loop(0, dil // streams, body, 0)

    lses = [lse_scr[g] for g in range(A_N_GROUPS)]
    mx = functools.reduce(jnp.maximum, lses)
    es = [jnp.exp2(v - mx) for v in lses]
    num = functools.reduce(lambda a, b: a + b, [es[g] * o_scr[g] for g in range(A_N_GROUPS)])
    den = functools.reduce(lambda a, b: a + b, es)
    out_ref[0] = (num * (1.0 / den)).astype(BF16)


def _attn_core(qkv, seq):
    bsz = qkv[0].shape[0]
    npair = A_GROUP_COLS // LANES
    in_specs = []
    for _, dil in A_GROUPS:
        for _ in range(3):
            in_specs.append(pl.BlockSpec((1, dil, seq // dil, LANES), lambda b, p: (b, 0, 0, p)))
    return pl.pallas_call(
        functools.partial(_attn_core_kernel, seq=seq),
        grid=(bsz, npair),
        in_specs=in_specs,
        out_specs=pl.BlockSpec((1, seq, LANES), lambda b, p: (b, 0, p)),
        out_shape=jax.ShapeDtypeStruct((bsz, seq, A_GROUP_COLS), BF16),
        scratch_shapes=[pltpu.VMEM((A_N_GROUPS, seq, LANES), F32),
                        pltpu.VMEM((A_N_GROUPS, seq, LANES), F32),
                        pltpu.VMEM((2 * A_SPAN, 2 * A_SPAN), F32),
                        pltpu.VMEM((2 * A_SPAN, A_SPAN), F32)],
        compiler_params=pltpu.CompilerParams(
            dimension_semantics=("parallel", "parallel"), vmem_limit_bytes=VMEM_LIMIT),
        name="attn_core",
    )(*qkv)


def _gla_proj_kernel(x_ref, g_ref, w_ref, wgd_ref, wgu_ref, bgu_ref, tri_ref,
                     q_ref, k_ref, v_ref, r_ref, b_ref, *, tm):
    h = _rms(x_ref[...], g_ref[...]).astype(BF16)
    q_ref[...] = (_dot(h, w_ref[:, :B_KEY_DIM]) * (B_DK ** -0.5)).astype(BF16)
    k_ref[...] = _dot(h, w_ref[:, B_KEY_DIM:2 * B_KEY_DIM]).astype(BF16)
    c0 = 2 * B_KEY_DIM
    v_ref[...] = _dot(h, w_ref[:, c0:c0 + B_VAL_DIM]).astype(BF16)
    r_ref[...] = _dot(h, w_ref[:, c0 + B_VAL_DIM:c0 + 2 * B_VAL_DIM])
    gd = _dot(h, wgd_ref[...]).astype(BF16)
    z = _dot(gd, wgu_ref[...]) + bgu_ref[...]
    log_sig = jnp.minimum(z, 0.0) - jnp.log(1.0 + jnp.exp(-jnp.abs(z)))
    glog = log_sig / B_GATE_NORMALIZER
    hi = glog.astype(BF16)
    rem = glog - hi.astype(F32)
    mid = rem.astype(BF16)
    lo = (rem - mid.astype(F32)).astype(BF16)
    tri = tri_ref[...]
    for s in range(tm // B_SLAB):
        rows = slice(s * B_SLAB, (s + 1) * B_SLAB)
        b_ref[rows, :] = _dot(tri, hi[rows]) + _dot(tri, mid[rows]) + _dot(tri, lo[rows])


def _gla_proj(x2, g, w_main, w_gd, w_gu, b_gu, layers, tm=512):
    t = x2.shape[0]
    li, lj = layers
    row = lambda i: (i, 0)
    idx = jnp.arange(B_SLAB)
    tri = ((idx[:, None] // B_CHUNK == idx[None, :] // B_CHUNK)
           & (idx[None, :] <= idx[:, None])).astype(BF16)
    return pl.pallas_call(
        functools.partial(_gla_proj_kernel, tm=tm),
        grid=(t // tm,),
        in_specs=[
            pl.BlockSpec((tm, D_MODEL), row),
            _resident((1, D_MODEL), li),
            _resident(w_main.shape[1:], lj),
            _resident(w_gd.shape[1:], lj),
            _resident(w_gu.shape[1:], lj),
            _resident(b_gu.shape[1:], lj),
            _resident(tri.shape),
        ],
        out_specs=[
            pl.BlockSpec((tm, B_KEY_DIM), row),
            pl.BlockSpec((tm, B_KEY_DIM), row),
            pl.BlockSpec((tm, B_VAL_DIM), row),
            pl.BlockSpec((tm, B_VAL_DIM), row),
            pl.BlockSpec((tm, B_KEY_DIM), row),
        ],
        out_shape=[
            jax.ShapeDtypeStruct((t, B_KEY_DIM), BF16),
            jax.ShapeDtypeStruct((t, B_KEY_DIM), BF16),
            jax.ShapeDtypeStruct((t, B_VAL_DIM), BF16),
            jax.ShapeDtypeStruct((t, B_VAL_DIM), F32),
            jax.ShapeDtypeStruct((t, B_KEY_DIM), F32),
        ],
        compiler_params=pltpu.CompilerParams(
            dimension_semantics=("parallel",), vmem_limit_bytes=VMEM_LIMIT),
        name="gla_proj",
    )(x2, g, w_main, w_gd, w_gu, b_gu, tri)


def _gla_core_kernel(q_ref, k_ref, v_ref, r_ref, b_ref, gn_ref, out_ref, *, seq, unroll):
    c = B_CHUNK
    ri = lax.broadcasted_iota(jnp.int32, (c, c), 0)
    ci = lax.broadcasted_iota(jnp.int32, (c, c), 1)
    gn = gn_ref[...]

    def local(n):
        base = pl.multiple_of(n * c, c)
        rows = pl.ds(base, c)
        bb = b_ref[0, rows, :]
        b_mid = b_ref[0, pl.ds(base + c // 2 - 1, 1), :]
        b_last = b_ref[0, pl.ds(base + c - 1, 1), :]
        qf = q_ref[0, rows, :].astype(F32)
        kf = k_ref[0, rows, :].astype(F32)
        vb = v_ref[0, rows, :]
        x_mid = bb - b_mid
        qe = (qf * jnp.exp(x_mid)).astype(BF16)
        ke = (kf * jnp.exp(-x_mid)).astype(BF16)
        a = jnp.where(ci <= ri, _dot_nt(qe, ke), 0.0).astype(BF16)
        qi = (qf * jnp.exp(bb)).astype(BF16)
        ks = (kf * jnp.exp(b_last - bb)).astype(BF16)
        return rows, a, vb, qi, _dot_tn(vb, ks), jnp.exp(b_last)

    def finish(st, rows, a, vb, qi, kv, decay):
        o = _dot(a, vb) + _dot_nt(qi, st.astype(BF16))
        o = _rms(o, gn)
        rr = r_ref[0, rows, :]
        out_ref[0, rows, :] = (o * (rr * (1.0 / (1.0 + jnp.exp(-rr))))).astype(BF16)
        return st * decay + kv

    def body(i, st):
        parts = [local(i * unroll + j) for j in range(unroll)]
        for part in parts:
            st = finish(st, *part)
        return st

    lax.fori_loop(0, seq // c // unroll, body, jnp.zeros((B_DV, B_DK), F32))


def _gla_core(q, k, v, r, b, g_norm, layer, seq, unroll=8):
    bsz = q.shape[0]
    hblock = lambda width: pl.BlockSpec((1, seq, width), lambda b, h: (b, 0, h))
    return pl.pallas_call(
        functools.partial(_gla_core_kernel, seq=seq, unroll=unroll),
        grid=(bsz, B_HEADS),
        in_specs=[hblock(B_DK), hblock(B_DK), hblock(B_DV), hblock(B_DV), hblock(B_DK),
                  _resident((1, B_DV), layer)],
        out_specs=hblock(B_DV),
        out_shape=jax.ShapeDtypeStruct((bsz, seq, B_VAL_DIM), BF16),
        compiler_params=pltpu.CompilerParams(
            dimension_semantics=("parallel", "parallel"), vmem_limit_bytes=VMEM_LIMIT),
        name="gla_core",
    )(q, k, v, r, b, g_norm)


def _ffn_kernel(x_ref, mix_ref, wo_ref, g_ref, win_ref, cw_ref, cb_ref, wd_ref, gf_ref,
                out_ref, carry_scr, act_scr, *, tm, final):
    i = pl.program_id(1)

    @pl.when(i == 0)
    def _():
        carry_scr[...] = jnp.zeros_like(carry_scr)

    x1 = x_ref[0] + _dot(mix_ref[0], wo_ref[...])
    h = _rms(x1, g_ref[...]).astype(BF16)
    row = lax.broadcasted_iota(jnp.int32, (tm, FFN_CHUNK), 0)
    for c in range(FFN_NCHUNK):
        cols = slice(c * FFN_CHUNK, (c + 1) * FFN_CHUNK)
        a = _dot(h, win_ref[:, cols])
        u = _dot(h, win_ref[:, FFN_DIM + c * FFN_CHUNK:FFN_DIM + (c + 1) * FFN_CHUNK])
        prev2 = carry_scr[6:7, cols]
        prev1 = carry_scr[7:8, cols]
        a1 = jnp.where(row == 0, prev1, pltpu.roll(a, 1, axis=0))
        a2 = jnp.where(row == 0, prev2, jnp.where(row == 1, prev1, pltpu.roll(a, 2, axis=0)))
        carry_scr[:, cols] = a[tm - 8:, :]
        conv = a * cw_ref[2:3, cols] + a1 * cw_ref[1:2, cols] + a2 * cw_ref[0:1, cols] + cb_ref[:, cols]
        act = conv * (1.0 / (1.0 + jnp.exp(-conv))) * u
        act_scr[:, cols] = act.astype(BF16)
    y = x1 + _dot(act_scr[...], wd_ref[...])
    if final:
        y = _rms(y, gf_ref[...])
    out_ref[0] = y


def _ffn(x, mix, w_o, g, w_in, conv_w, conv_b, w_down, g_final, layers, final, tm=1024):
    bsz, seq, _ = x.shape
    kmix = mix.shape[-1]
    li, lj = layers
    return pl.pallas_call(
        functools.partial(_ffn_kernel, tm=tm, final=final),
        grid=(bsz, seq // tm),
        in_specs=[
            pl.BlockSpec((1, tm, D_MODEL), lambda b, i: (b, i, 0)),
            pl.BlockSpec((1, tm, kmix), lambda b, i: (b, i, 0)),
            _resident(w_o.shape[1:], lj),
            _resident((1, D_MODEL), li),
            _resident(w_in.shape[1:], li),
            _resident(conv_w.shape[1:], li),
            _resident(conv_b.shape[1:], li),
            _resident(w_down.shape[1:], li),
            _resident((1, D_MODEL)),
        ],
        out_specs=pl.BlockSpec((1, tm, D_MODEL), lambda b, i: (b, i, 0)),
        out_shape=jax.ShapeDtypeStruct(x.shape, F32),
        scratch_shapes=[pltpu.VMEM((8, FFN_DIM), F32), pltpu.VMEM((tm, FFN_DIM), BF16)],
        compiler_params=pltpu.CompilerParams(
            dimension_semantics=("arbitrary", "arbitrary"), vmem_limit_bytes=VMEM_LIMIT),
        name="ffn",
    )(x, mix, w_o, g, w_in, conv_w, conv_b, w_down, g_final)


def kernel(x, positions, norm_mix, norm_ffn, a_w_qkv, a_w_o, b_w_in, b_w_gate_up,
           b_b_gate_up, b_g_norm, b_w_o, f_w_in, f_conv_w, f_conv_b, f_w_down, norm_final):
    bsz, seq, _ = x.shape
    depth = norm_mix.shape[0]
    g_mix = norm_mix[:, None, :]
    g_ffn = norm_ffn[:, None, :]
    g_final = norm_final[None, :]
    a_w = a_w_qkv.astype(BF16)
    a_wo = a_w_o.astype(BF16)
    ncore = 2 * B_KEY_DIM + 2 * B_VAL_DIM
    b_w = b_w_in.astype(BF16)
    b_wgd = jnp.pad(b_w_in[:, :, ncore:], ((0, 0), (0, 0), (0, LANES - B_GATE_RANK))).astype(BF16)
    b_wgu = jnp.pad(b_w_gate_up, ((0, 0), (0, LANES - B_GATE_RANK), (0, 0))).astype(BF16)
    b_bgu = b_b_gate_up[:, None, :]
    b_gn = b_g_norm[:, None, :]
    b_wo = b_w_o.astype(BF16)
    f_win = f_w_in.astype(BF16)
    f_wd = f_w_down.astype(BF16)
    f_cb = f_conv_b[:, None, :]

    tables = _rope_tables(positions)
    for i in range(depth):
        j = i // 2
        if i % 2 == 0:
            qkv = _attn_proj(x, g_mix, a_w, (i, j), tables)
            mix = _attn_core(qkv, seq)
            w_o = a_wo
        else:
            q, k, v, r, b = _gla_proj(x.reshape(bsz * seq, D_MODEL), g_mix, b_w, b_wgd, b_wgu,
                                      b_bgu, (i, j))
            shp = lambda t: t.reshape(bsz, seq, t.shape[-1])
            mix = _gla_core(shp(q), shp(k), shp(v), shp(r), shp(b), b_gn, j, seq)
            w_o = b_wo
        x = _ffn(x, mix, w_o, g_ffn, f_win, f_conv_w, f_cb, f_wd, g_final, (i, j),
                 final=(i == depth - 1))
    return x
```

```python
import functools

import jax
import jax.numpy as jnp
from jax import lax
from jax.experimental import pallas as pl
from jax.experimental.pallas import tpu as pltpu

D_MODEL = 1024
RMS_EPS = 1e-6

A_GROUPS = ((128, 1), (512, 4), (2048, 16))
A_N_GROUPS = len(A_GROUPS)
A_HEADS_PER_GROUP = 8
A_HEAD_DIM = 64
A_GROUP_COLS = A_HEADS_PER_GROUP * A_HEAD_DIM
A_SPAN = 128
A_BLOCKS_PER_BODY = (15, 16, 16)
A_SCORE_LOOKAHEAD = 3
ROPE_THETA = 10000.0
NEG_INF = -1e30
LOG2_E = 1.4426950408889634

B_HEADS = 4
B_KEY_DIM = 512
B_VAL_DIM = 1024
B_DK = B_KEY_DIM // B_HEADS
B_DV = B_VAL_DIM // B_HEADS
B_GATE_RANK = 16
B_GATE_NORMALIZER = 16.0
B_CHUNK = 64
B_SLAB = 256
B_LOCAL_LOOKAHEAD = 3

FFN_DIM = 2816
FFN_CHUNK = 256
FFN_NCHUNK = FFN_DIM // FFN_CHUNK
CONV_WIDTH = 3

LANES = 128
VMEM_LIMIT = 52 * 1024 * 1024

BF16 = jnp.bfloat16
F32 = jnp.float32


def _resident(shape, layer=None):
    nd = len(shape)
    if layer is None:
        return pl.BlockSpec(shape, lambda *_: (0,) * nd, pipeline_mode=pl.Buffered(1))
    return pl.BlockSpec((None,) + tuple(shape), lambda *_: (layer,) + (0,) * nd,
                        pipeline_mode=pl.Buffered(1))


def _rms(x, g):
    ms = jnp.mean(x * x, axis=-1, keepdims=True)
    return x * lax.rsqrt(ms + RMS_EPS) * g


def _dot(a, b):
    return jnp.dot(a, b, preferred_element_type=F32)


def _dot_nt(a, b):
    return lax.dot_general(a, b, (((1,), (1,)), ((), ())), preferred_element_type=F32)


def _dot_tn(a, b):
    return lax.dot_general(a, b, (((0,), (0,)), ((), ())), preferred_element_type=F32)


def _rope_table_kernel(pos_ref, inv_ref, cos_ref, sin_lo_ref, sin_hi_ref):
    ang = pos_ref[...].astype(F32) * inv_ref[...]
    cos_ref[...] = jnp.cos(ang)
    sin = jnp.sin(ang)
    lane = lax.broadcasted_iota(jnp.int32, sin.shape, 1)
    is_lo = (lane % A_HEAD_DIM) < (A_HEAD_DIM // 2)
    sin_lo_ref[...] = jnp.where(is_lo, -sin, 0.0)
    sin_hi_ref[...] = jnp.where(is_lo, 0.0, sin)


def _rope_tables(positions):
    t = positions.size
    tm = min(t, 2048)
    half = A_HEAD_DIM // 2
    inv = ROPE_THETA ** (-jnp.arange(0, A_HEAD_DIM, 2, dtype=F32) / A_HEAD_DIM)
    inv = jnp.tile(inv, LANES // half)[None, :]
    return pl.pallas_call(
        _rope_table_kernel,
        grid=(t // tm,),
        in_specs=[
            pl.BlockSpec((tm, 1), lambda i: (i, 0)),
            pl.BlockSpec((1, LANES), lambda i: (0, 0)),
        ],
        out_specs=[pl.BlockSpec((tm, LANES), lambda i: (i, 0))] * 3,
        out_shape=[jax.ShapeDtypeStruct((t, LANES), F32)] * 3,
        name="rope_tables",
    )(positions.reshape(t, 1), inv)


def _attn_proj_kernel(x_ref, g_ref, w_ref, cos_ref, sin_lo_ref, sin_hi_ref, *refs, tm):
    outs = refs[: 3 * A_N_GROUPS]
    scr = refs[3 * A_N_GROUPS]
    h = _rms(x_ref[0], g_ref[...]).astype(BF16)
    cos = cos_ref[...]
    sin_lo = sin_lo_ref[...]
    sin_hi = sin_hi_ref[...]
    half = A_HEAD_DIM // 2

    def project(part, g):
        c0 = (part * A_N_GROUPS + g) * A_GROUP_COLS
        return _dot(h, w_ref[:, c0:c0 + A_GROUP_COLS])

    def emit(part, g, y, slot):
        dil = A_GROUPS[g][1]
        o_ref = outs[g * 3 + part]
        for j in range(A_GROUP_COLS // LANES):
            lanes = slice(j * LANES, (j + 1) * LANES)
            yb = y[:, lanes]
            if part < 2:
                yb = (yb * cos + pltpu.roll(yb, LANES - half, axis=1) * sin_lo
                      + pltpu.roll(yb, half, axis=1) * sin_hi)
                if part == 0:
                    yb = yb * (A_HEAD_DIM ** -0.5 * LOG2_E)
            if dil == 1:
                o_ref[0, 0, :, lanes] = yb.astype(BF16)
            else:
                scr[slot, j] = yb
                for r in range(dil):
                    o_ref[0, r, :, lanes] = scr[slot, j, pl.ds(r, tm // dil, stride=dil), :].astype(BF16)

    order = [(0, 2), (1, 2), (0, 1), (1, 1), (2, 2), (2, 1), (0, 0), (1, 0), (2, 0)]
    y_next = project(*order[0])
    for idx, (part, g) in enumerate(order):
        y = y_next
        if idx + 1 < len(order):
            y_next = project(*order[idx + 1])
        emit(part, g, y, idx % 2)


def _attn_proj(x, g, w, layer, tables, tm=1024):
    bsz, seq, _ = x.shape
    nt = seq // tm
    out_shapes, out_specs = [], []
    for _, dil in A_GROUPS:
        for _ in range(3):
            out_shapes.append(jax.ShapeDtypeStruct((bsz, dil, seq // dil, A_GROUP_COLS), BF16))
            out_specs.append(pl.BlockSpec((1, dil, tm // dil, A_GROUP_COLS), lambda b, i: (b, 0, i, 0)))
    return pl.pallas_call(
        functools.partial(_attn_proj_kernel, tm=tm),
        grid=(bsz, nt),
        in_specs=[
            pl.BlockSpec((1, tm, D_MODEL), lambda b, i: (b, i, 0)),
            _resident((1, D_MODEL), layer[0]),
            _resident(w.shape[1:], layer[1]),
        ] + [pl.BlockSpec((tm, LANES), lambda b, i: (b * nt + i, 0))] * len(tables),
        out_specs=out_specs,
        out_shape=out_shapes,
        scratch_shapes=[pltpu.VMEM((2, A_GROUP_COLS // LANES, tm, LANES), F32)],
        compiler_params=pltpu.CompilerParams(
            dimension_semantics=("parallel", "parallel"), vmem_limit_bytes=VMEM_LIMIT),
        name="attn_proj",
    )(x, g, w, *tables)


def _attn_core_kernel(*refs, seq):
    qkv = refs[: 3 * A_N_GROUPS]
    out_ref = refs[3 * A_N_GROUPS]
    o_scr, lse_scr, band_scr, causal_scr = refs[3 * A_N_GROUPS + 1:]
    span = A_SPAN
    lane = lax.broadcasted_iota(jnp.int32, (1, LANES), 1)
    v_is_a = lane < (LANES // 2)
    qk_is_a = v_is_a

    row = lax.broadcasted_iota(jnp.int32, (2 * span, 2 * span), 0) % span
    col = lax.broadcasted_iota(jnp.int32, (2 * span, 2 * span), 1)
    band_scr[...] = jnp.where((col >= row) & (col <= row + span), 0.0, NEG_INF)
    row1 = lax.broadcasted_iota(jnp.int32, (2 * span, span), 0) % span
    col1 = lax.broadcasted_iota(jnp.int32, (2 * span, span), 1)
    causal_scr[...] = jnp.where(col1 <= row1, 0.0, NEG_INF)

    def scores(g, r, n, first):
        q_ref, k_ref, _ = qkv[3 * g: 3 * g + 3]
        if first:
            qb = q_ref[0, r, pl.ds(0, span), :]
            kb = k_ref[0, r, pl.ds(0, span), :]
            bias = causal_scr[...]
        else:
            k0 = pl.multiple_of((n - 1) * span, span)
            qb = q_ref[0, r, pl.ds(k0 + span, span), :]
            kb = k_ref[0, r, pl.ds(k0, 2 * span), :]
            bias = band_scr[...]
        zero = jnp.zeros_like(qb)
        q2 = jnp.concatenate([jnp.where(qk_is_a, qb, zero), jnp.where(qk_is_a, zero, qb)], axis=0)
        return _dot_nt(q2, kb) + bias

    def finish(g, dil, r, n, first, s):
        v_ref = qkv[3 * g + 2]
        if first:
            vb = v_ref[0, r, pl.ds(0, span), :]
        else:
            vb = v_ref[0, r, pl.ds(pl.multiple_of((n - 1) * span, span), 2 * span), :]
        m_a = jnp.max(s[:span], axis=-1, keepdims=True)
        m_b = jnp.max(s[span:], axis=-1, keepdims=True)
        p = jnp.concatenate([jnp.exp2(s[:span] - m_a), jnp.exp2(s[span:] - m_b)], axis=1).astype(BF16)
        zero = jnp.zeros_like(vb)
        ind_a = jnp.broadcast_to(jnp.where(v_is_a, 1.0, 0.0), vb.shape).astype(BF16)
        ind_b = jnp.broadcast_to(jnp.where(v_is_a, 0.0, 1.0), vb.shape).astype(BF16)
        rhs = jnp.concatenate([
            jnp.concatenate([jnp.where(v_is_a, vb, zero), ind_a], axis=1),
            jnp.concatenate([jnp.where(v_is_a, zero, vb), ind_b], axis=1),
        ], axis=0)
        res = _dot(p, rhs)
        l = res[:, LANES:]
        o = res[:, :LANES] * (1.0 / l)
        lse = jnp.where(v_is_a, m_a, m_b) + jnp.log2(l)
        start = n * span * dil + r
        if dil == 1:
            rows = pl.ds(pl.multiple_of(start, span), span)
        else:
            rows = pl.ds(start, span, stride=dil)
        o_scr[g, rows, :] = o
        lse_scr[g, rows, :] = lse

    def blocks(g, dil, todo):
        pending = []
        for blk in todo:
            pending.append((blk, scores(g, *blk)))
            if len(pending) > A_SCORE_LOOKAHEAD:
                (r, n, first), s = pending.pop(0)
                finish(g, dil, r, n, first, s)
        for (r, n, first), s in pending:
            finish(g, dil, r, n, first, s)

    for g, (_, dil) in enumerate(A_GROUPS):
        nb = seq // dil // span
        if dil == 1:
            blocks(g, dil, [(0, 0, True)])
            per = A_BLOCKS_PER_BODY[g]
            def body(i, carry, g=g, dil=dil, per=per):
                blocks(g, dil, [(0, 1 + i * per + j, False) for j in range(per)])
                return carry
            lax.fori_loop(0, (nb - 1) // per, body, 0)
        else:
            streams = A_BLOCKS_PER_BODY[g] // nb
            def body(i, carry, g=g, dil=dil, nb=nb, streams=streams):
                blocks(g, dil, [(i * streams + j, n, n == 0)
                                for j in range(streams) for n in range(nb)])
                return carry
            lax.fori_loop(0, dil // streams, body, 0)

    lses = [lse_scr[g] for g in range(A_N_GROUPS)]
    mx = functools.reduce(jnp.maximum, lses)
    es = [jnp.exp2(v - mx) for v in lses]
    num = functools.reduce(lambda a, b: a + b, [es[g] * o_scr[g] for g in range(A_N_GROUPS)])
    den = functools.reduce(lambda a, b: a + b, es)
    out_ref[0] = (num * (1.0 / den)).astype(BF16)


def _attn_core(qkv, seq):
    bsz = qkv[0].shape[0]
    npair = A_GROUP_COLS // LANES
    in_specs = []
    for _, dil in A_GROUPS:
        for _ in range(3):
            in_specs.append(pl.BlockSpec((1, dil, seq // dil, LANES), lambda b, p: (b, 0, 0, p)))
    return pl.pallas_call(
        functools.partial(_attn_core_kernel, seq=seq),
        grid=(bsz, npair),
        in_specs=in_specs,
        out_specs=pl.BlockSpec((1, seq, LANES), lambda b, p: (b, 0, p)),
        out_shape=jax.ShapeDtypeStruct((bsz, seq, A_GROUP_COLS), BF16),
        scratch_shapes=[pltpu.VMEM((A_N_GROUPS, seq, LANES), F32),
                        pltpu.VMEM((A_N_GROUPS, seq, LANES), F32),
                        pltpu.VMEM((2 * A_SPAN, 2 * A_SPAN), F32),
                        pltpu.VMEM((2 * A_SPAN, A_SPAN), F32)],
        compiler_params=pltpu.CompilerParams(
            dimension_semantics=("parallel", "parallel"), vmem_limit_bytes=VMEM_LIMIT),
        name="attn_core",
    )(*qkv)


def _gla_proj_kernel(x_ref, g_ref, w_ref, wgd_ref, wgu_ref, bgu_ref, tri_ref,
                     q_ref, k_ref, v_ref, r_ref, b_ref, *, tm):
    h = _rms(x_ref[...], g_ref[...]).astype(BF16)
    gd = _dot(h, wgd_ref[...]).astype(BF16)
    z = _dot(gd, wgu_ref[...]) + bgu_ref[...]
    q_ref[...] = (_dot(h, w_ref[:, :B_KEY_DIM]) * (B_DK ** -0.5)).astype(BF16)
    k_ref[...] = _dot(h, w_ref[:, B_KEY_DIM:2 * B_KEY_DIM]).astype(BF16)
    log_sig = jnp.minimum(z, 0.0) - jnp.log(1.0 + jnp.exp(-jnp.abs(z)))
    glog = log_sig * (LOG2_E / B_GATE_NORMALIZER)
    hi = glog.astype(BF16)
    rem = glog - hi.astype(F32)
    mid = rem.astype(BF16)
    lo = (rem - mid.astype(F32)).astype(BF16)
    tri = tri_ref[...]
    for s in range(tm // B_SLAB):
        rows = slice(s * B_SLAB, (s + 1) * B_SLAB)
        b_ref[rows, :] = _dot(tri, hi[rows]) + _dot(tri, mid[rows]) + _dot(tri, lo[rows])
    c0 = 2 * B_KEY_DIM
    v_ref[...] = _dot(h, w_ref[:, c0:c0 + B_VAL_DIM]).astype(BF16)
    r = _dot(h, w_ref[:, c0 + B_VAL_DIM:c0 + 2 * B_VAL_DIM])
    r_ref[...] = (r * (1.0 / (1.0 + jnp.exp2(r * -LOG2_E)))).astype(BF16)


def _gla_proj(x2, g, w_main, w_gd, w_gu, b_gu, layers, tm=512):
    t = x2.shape[0]
    li, lj = layers
    row = lambda i: (i, 0)
    idx = jnp.arange(B_SLAB)
    tri = ((idx[:, None] // B_CHUNK == idx[None, :] // B_CHUNK)
           & (idx[None, :] <= idx[:, None])).astype(BF16)
    return pl.pallas_call(
        functools.partial(_gla_proj_kernel, tm=tm),
        grid=(t // tm,),
        in_specs=[
            pl.BlockSpec((tm, D_MODEL), row),
            _resident((1, D_MODEL), li),
            _resident(w_main.shape[1:], lj),
            _resident(w_gd.shape[1:], lj),
            _resident(w_gu.shape[1:], lj),
            _resident(b_gu.shape[1:], lj),
            _resident(tri.shape),
        ],
        out_specs=[
            pl.BlockSpec((tm, B_KEY_DIM), row),
            pl.BlockSpec((tm, B_KEY_DIM), row),
            pl.BlockSpec((tm, B_VAL_DIM), row),
            pl.BlockSpec((tm, B_VAL_DIM), row),
            pl.BlockSpec((tm, B_KEY_DIM), row),
        ],
        out_shape=[
            jax.ShapeDtypeStruct((t, B_KEY_DIM), BF16),
            jax.ShapeDtypeStruct((t, B_KEY_DIM), BF16),
            jax.ShapeDtypeStruct((t, B_VAL_DIM), BF16),
            jax.ShapeDtypeStruct((t, B_VAL_DIM), BF16),
            jax.ShapeDtypeStruct((t, B_KEY_DIM), F32),
        ],
        compiler_params=pltpu.CompilerParams(
            dimension_semantics=("parallel",), vmem_limit_bytes=VMEM_LIMIT),
        name="gla_proj",
    )(x2, g, w_main, w_gd, w_gu, b_gu, tri)


def _gla_core_kernel(q_ref, k_ref, v_ref, r_ref, b_ref, gn_ref, out_ref, *, seq, unroll):
    c = B_CHUNK
    ri = lax.broadcasted_iota(jnp.int32, (c, c), 0)
    ci = lax.broadcasted_iota(jnp.int32, (c, c), 1)
    gn = gn_ref[...]

    def local(n):
        base = pl.multiple_of(n * c, c)
        rows = pl.ds(base, c)
        bb = b_ref[0, rows, :]
        b_mid = b_ref[0, pl.ds(base + c // 2 - 1, 1), :]
        b_last = b_ref[0, pl.ds(base + c - 1, 1), :]
        qf = q_ref[0, rows, :].astype(F32)
        kf = k_ref[0, rows, :].astype(F32)
        vb = v_ref[0, rows, :]
        x_mid = bb - b_mid
        qe = (qf * jnp.exp2(x_mid)).astype(BF16)
        ke = (kf * jnp.exp2(-x_mid)).astype(BF16)
        a = jnp.where(ci <= ri, _dot_nt(qe, ke), 0.0).astype(BF16)
        qi = (qf * jnp.exp2(bb)).astype(BF16)
        ks = (kf * jnp.exp2(b_last - bb)).astype(BF16)
        return rows, a, vb, qi, _dot_tn(vb, ks), jnp.exp2(b_last)

    def finish(st, rows, a, vb, qi, kv, decay):
        o = _dot(a, vb) + _dot_nt(qi, st.astype(BF16))
        o = _rms(o, gn)
        out_ref[0, rows, :] = (o * r_ref[0, rows, :].astype(F32)).astype(BF16)
        return st * decay + kv

    def body(i, st):
        pending = []
        for j in range(unroll):
            pending.append(local(i * unroll + j))
            if len(pending) > B_LOCAL_LOOKAHEAD:
                st = finish(st, *pending.pop(0))
        for part in pending:
            st = finish(st, *part)
        return st

    lax.fori_loop(0, seq // c // unroll, body, jnp.zeros((B_DV, B_DK), F32))


def _gla_core(q, k, v, r, b, g_norm, layer, seq, unroll=32):
    bsz = q.shape[0]
    hblock = lambda width: pl.BlockSpec((1, seq, width), lambda b, h: (b, 0, h))
    return pl.pallas_call(
        functools.partial(_gla_core_kernel, seq=seq, unroll=unroll),
        grid=(bsz, B_HEADS),
        in_specs=[hblock(B_DK), hblock(B_DK), hblock(B_DV), hblock(B_DV), hblock(B_DK),
                  _resident((1, B_DV), layer)],
        out_specs=hblock(B_DV),
        out_shape=jax.ShapeDtypeStruct((bsz, seq, B_VAL_DIM), BF16),
        compiler_params=pltpu.CompilerParams(
            dimension_semantics=("parallel", "parallel"), vmem_limit_bytes=VMEM_LIMIT),
        name="gla_core",
    )(q, k, v, r, b, g_norm)


def _ffn_kernel(x_ref, mix_ref, wo_ref, g_ref, win_ref, cw_ref, cb_ref, wd_ref, gf_ref,
                out_ref, carry_scr, act_scr, *, tm, nsub, final):
    i = pl.program_id(1)

    @pl.when(i == 0)
    def _():
        carry_scr[...] = jnp.zeros_like(carry_scr)

    ts = tm // nsub
    subs = [slice(s * ts, (s + 1) * ts) for s in range(nsub)]
    x1 = [x_ref[0, rows, :] + _dot(mix_ref[0, rows, :], wo_ref[...]) for rows in subs]
    h = [_rms(v, g_ref[...]).astype(BF16) for v in x1]
    row = lax.broadcasted_iota(jnp.int32, (ts, FFN_CHUNK), 0)
    for s, rows in enumerate(subs):
        for c in range(FFN_NCHUNK):
            cols = slice(c * FFN_CHUNK, (c + 1) * FFN_CHUNK)
            a = _dot(h[s], win_ref[:, cols])
            u = _dot(h[s], win_ref[:, FFN_DIM + c * FFN_CHUNK:FFN_DIM + (c + 1) * FFN_CHUNK])
            prev2 = carry_scr[6:7, cols]
            prev1 = carry_scr[7:8, cols]
            a1 = jnp.where(row == 0, prev1, pltpu.roll(a, 1, axis=0))
            a2 = jnp.where(row == 0, prev2, jnp.where(row == 1, prev1, pltpu.roll(a, 2, axis=0)))
            carry_scr[:, cols] = a[ts - 8:, :]
            conv = (a * cw_ref[2:3, cols] + a1 * cw_ref[1:2, cols] + a2 * cw_ref[0:1, cols]
                    + cb_ref[:, cols])
            act = conv * (1.0 / (1.0 + jnp.exp2(conv * -LOG2_E))) * u
            act_scr[rows, cols] = act.astype(BF16)
        y = x1[s] + _dot(act_scr[rows, :], wd_ref[...])
        if final:
            y = _rms(y, gf_ref[...])
        out_ref[0, rows, :] = y


def _ffn(x, mix, w_o, g, w_in, conv_w, conv_b, w_down, g_final, layers, final, tm=1024, nsub=2):
    bsz, seq, _ = x.shape
    kmix = mix.shape[-1]
    li, lj = layers
    return pl.pallas_call(
        functools.partial(_ffn_kernel, tm=tm, nsub=nsub, final=final),
        grid=(bsz, seq // tm),
        in_specs=[
            pl.BlockSpec((1, tm, D_MODEL), lambda b, i: (b, i, 0)),
            pl.BlockSpec((1, tm, kmix), lambda b, i: (b, i, 0)),
            _resident(w_o.shape[1:], lj),
            _resident((1, D_MODEL), li),
            _resident(w_in.shape[1:], li),
            _resident(conv_w.shape[1:], li),
            _resident(conv_b.shape[1:], li),
            _resident(w_down.shape[1:], li),
            _resident((1, D_MODEL)),
        ],
        out_specs=pl.BlockSpec((1, tm, D_MODEL), lambda b, i: (b, i, 0)),
        out_shape=jax.ShapeDtypeStruct(x.shape, F32),
        scratch_shapes=[pltpu.VMEM((8, FFN_DIM), F32), pltpu.VMEM((tm, FFN_DIM), BF16)],
        compiler_params=pltpu.CompilerParams(
            dimension_semantics=("arbitrary", "arbitrary"), vmem_limit_bytes=VMEM_LIMIT),
        name="ffn",
    )(x, mix, w_o, g, w_in, conv_w, conv_b, w_down, g_final)


def kernel(x, positions, norm_mix, norm_ffn, a_w_qkv, a_w_o, b_w_in, b_w_gate_up,
           b_b_gate_up, b_g_norm, b_w_o, f_w_in, f_conv_w, f_conv_b, f_w_down, norm_final):
    bsz, seq, _ = x.shape
    depth = norm_mix.shape[0]
    g_mix = norm_mix[:, None, :]
    g_ffn = norm_ffn[:, None, :]
    g_final = norm_final[None, :]
    a_w = a_w_qkv.astype(BF16)
    a_wo = a_w_o.astype(BF16)
    ncore = 2 * B_KEY_DIM + 2 * B_VAL_DIM
    b_w = b_w_in.astype(BF16)
    b_wgd = jnp.pad(b_w_in[:, :, ncore:], ((0, 0), (0, 0), (0, LANES - B_GATE_RANK))).astype(BF16)
    b_wgu = jnp.pad(b_w_gate_up, ((0, 0), (0, LANES - B_GATE_RANK), (0, 0))).astype(BF16)
    b_bgu = b_b_gate_up[:, None, :]
    b_gn = b_g_norm[:, None, :]
    b_wo = b_w_o.astype(BF16)
    f_win = f_w_in.astype(BF16)
    f_wd = f_w_down.astype(BF16)
    f_cb = f_conv_b[:, None, :]

    tables = _rope_tables(positions)
    for i in range(depth):
        j = i // 2
        if i % 2 == 0:
            qkv = _attn_proj(x, g_mix, a_w, (i, j), tables)
            mix = _attn_core(qkv, seq)
            w_o = a_wo
        else:
            q, k, v, r, b = _gla_proj(x.reshape(bsz * seq, D_MODEL), g_mix, b_w, b_wgd, b_wgu,
                                      b_bgu, (i, j))
            shp = lambda t: t.reshape(bsz, seq, t.shape[-1])
            mix = _gla_core(shp(q), shp(k), shp(v), shp(r), shp(b), b_gn, j, seq)
            w_o = b_wo
        x = _ffn(x, mix, w_o, g_ffn, f_win, f_conv_w, f_cb, f_wd, g_final, (i, j),
                 final=(i == depth - 1))
    return x
```

```python
import functools

import jax
import jax.numpy as jnp
from jax import lax
from jax.experimental import pallas as pl
from jax.experimental.pallas import tpu as pltpu

D_MODEL = 1024
RMS_EPS = 1e-6

A_GROUPS = ((128, 1), (512, 4), (2048, 16))
A_N_GROUPS = len(A_GROUPS)
A_HEADS_PER_GROUP = 8
A_HEAD_DIM = 64
A_GROUP_COLS = A_HEADS_PER_GROUP * A_HEAD_DIM
A_SPAN = 128
A_BLOCKS_PER_BODY = (15, 16, 16)
A_SCORE_LOOKAHEAD = 3
ROPE_THETA = 10000.0
NEG_INF = -1e30
LOG2_E = 1.4426950408889634

B_HEADS = 4
B_KEY_DIM = 512
B_VAL_DIM = 1024
B_DK = B_KEY_DIM // B_HEADS
B_DV = B_VAL_DIM // B_HEADS
B_GATE_RANK = 16
B_GATE_NORMALIZER = 16.0
B_CHUNK = 64
B_SLAB = 256
B_LOCAL_LOOKAHEAD = 4

FFN_DIM = 2816
FFN_CHUNK = 256
FFN_NCHUNK = FFN_DIM // FFN_CHUNK
CONV_WIDTH = 3

LANES = 128
ROPE_PACK = LANES // (A_HEAD_DIM // 2)
VMEM_LIMIT = 52 * 1024 * 1024

BF16 = jnp.bfloat16
F32 = jnp.float32


def _resident(shape, layer=None):
    nd = len(shape)
    if layer is None:
        return pl.BlockSpec(shape, lambda *_: (0,) * nd, pipeline_mode=pl.Buffered(1))
    return pl.BlockSpec((None,) + tuple(shape), lambda *_: (layer,) + (0,) * nd,
                        pipeline_mode=pl.Buffered(1))


def _rms(x, g):
    ms = jnp.mean(x * x, axis=-1, keepdims=True)
    return x * lax.rsqrt(ms + RMS_EPS) * g


def _dot(a, b):
    return jnp.dot(a, b, preferred_element_type=F32)


def _dot_nt(a, b):
    return lax.dot_general(a, b, (((1,), (1,)), ((), ())), preferred_element_type=F32)


def _dot_tn(a, b):
    return lax.dot_general(a, b, (((0,), (0,)), ((), ())), preferred_element_type=F32)


def _rope_table_kernel(pos_ref, inv_ref, cos_ref, sin_lo_ref, sin_hi_ref, *, tq):
    half = A_HEAD_DIM // 2
    ang = pos_ref[...].astype(F32) * inv_ref[...]
    lane = lax.broadcasted_iota(jnp.int32, ang.shape, 1)
    group = lane // half
    is_lo = (lane % A_HEAD_DIM) < half
    for src, dsts in ((jnp.cos(ang), (cos_ref,)), (jnp.sin(ang), (sin_lo_ref, sin_hi_ref))):
        rolled = [src] + [pltpu.roll(src, half * k, axis=1) for k in range(1, ROPE_PACK)]
        for c in range(ROPE_PACK):
            val = rolled[(ROPE_PACK - 1 - c) % ROPE_PACK]
            for grp in range(ROPE_PACK - 2, -1, -1):
                val = jnp.where(group == grp, rolled[(grp - c) % ROPE_PACK], val)
            rows = pl.ds(c, tq, stride=ROPE_PACK)
            if len(dsts) == 1:
                dsts[0][rows, :] = val
            else:
                dsts[0][rows, :] = jnp.where(is_lo, -val, 0.0)
                dsts[1][rows, :] = jnp.where(is_lo, 0.0, val)


def _rope_tables(positions):
    t = positions.size
    tm = min(t, 2048)
    tq = tm // ROPE_PACK
    half = A_HEAD_DIM // 2
    inv = ROPE_THETA ** (-jnp.arange(0, A_HEAD_DIM, 2, dtype=F32) / A_HEAD_DIM)
    inv = jnp.tile(inv, ROPE_PACK)[None, :]
    pos = jnp.repeat(positions.reshape(t // ROPE_PACK, ROPE_PACK), half, axis=1)
    return pl.pallas_call(
        functools.partial(_rope_table_kernel, tq=tq),
        grid=(t // tm,),
        in_specs=[
            pl.BlockSpec((tq, LANES), lambda i: (i, 0)),
            pl.BlockSpec((1, LANES), lambda i: (0, 0)),
        ],
        out_specs=[pl.BlockSpec((tm, LANES), lambda i: (i, 0))] * 3,
        out_shape=[jax.ShapeDtypeStruct((t, LANES), F32)] * 3,
        name="rope_tables",
    )(pos, inv)


def _attn_proj_kernel(x_ref, g_ref, w_ref, cos_ref, sin_lo_ref, sin_hi_ref, *refs, tm):
    outs = refs[: 3 * A_N_GROUPS]
    scr = refs[3 * A_N_GROUPS]
    h = _rms(x_ref[0], g_ref[...]).astype(BF16)
    cos = cos_ref[...]
    sin_lo = sin_lo_ref[...]
    sin_hi = sin_hi_ref[...]
    half = A_HEAD_DIM // 2

    def project(part, g):
        c0 = (part * A_N_GROUPS + g) * A_GROUP_COLS
        return _dot(h, w_ref[:, c0:c0 + A_GROUP_COLS])

    def emit(part, g, y, slot):
        dil = A_GROUPS[g][1]
        o_ref = outs[g * 3 + part]
        for j in range(A_GROUP_COLS // LANES):
            lanes = slice(j * LANES, (j + 1) * LANES)
            yb = y[:, lanes]
            if part < 2:
                yb = (yb * cos + pltpu.roll(yb, LANES - half, axis=1) * sin_lo
                      + pltpu.roll(yb, half, axis=1) * sin_hi)
                if part == 0:
                    yb = yb * (A_HEAD_DIM ** -0.5 * LOG2_E)
            if dil == 1:
                o_ref[0, 0, :, lanes] = yb.astype(BF16)
            else:
                scr[slot, j] = yb
                for r in range(dil):
                    o_ref[0, r, :, lanes] = scr[slot, j, pl.ds(r, tm // dil, stride=dil), :].astype(BF16)

    order = [(0, 2), (1, 2), (0, 1), (1, 1), (2, 2), (2, 1), (0, 0), (1, 0), (2, 0)]
    y_next = project(*order[0])
    for idx, (part, g) in enumerate(order):
        y = y_next
        if idx + 1 < len(order):
            y_next = project(*order[idx + 1])
        emit(part, g, y, idx % 2)


def _attn_proj(x, g, w, layer, tables, tm=1024):
    bsz, seq, _ = x.shape
    nt = seq // tm
    out_shapes, out_specs = [], []
    for _, dil in A_GROUPS:
        for _ in range(3):
            out_shapes.append(jax.ShapeDtypeStruct((bsz, dil, seq // dil, A_GROUP_COLS), BF16))
            out_specs.append(pl.BlockSpec((1, dil, tm // dil, A_GROUP_COLS), lambda b, i: (b, 0, i, 0)))
    return pl.pallas_call(
        functools.partial(_attn_proj_kernel, tm=tm),
        grid=(bsz, nt),
        in_specs=[
            pl.BlockSpec((1, tm, D_MODEL), lambda b, i: (b, i, 0)),
            _resident((1, D_MODEL), layer[0]),
            _resident(w.shape[1:], layer[1]),
        ] + [pl.BlockSpec((tm, LANES), lambda b, i: (b * nt + i, 0))] * len(tables),
        out_specs=out_specs,
        out_shape=out_shapes,
        scratch_shapes=[pltpu.VMEM((2, A_GROUP_COLS // LANES, tm, LANES), F32)],
        compiler_params=pltpu.CompilerParams(
            dimension_semantics=("parallel", "parallel"), vmem_limit_bytes=VMEM_LIMIT),
        name="attn_proj",
    )(x, g, w, *tables)


def _attn_core_kernel(*refs, seq):
    qkv = refs[: 3 * A_N_GROUPS]
    out_ref = refs[3 * A_N_GROUPS]
    o_scr, lse_scr, band_scr, causal_scr = refs[3 * A_N_GROUPS + 1:]
    span = A_SPAN
    lane = lax.broadcasted_iota(jnp.int32, (1, LANES), 1)
    v_is_a = lane < (LANES // 2)
    qk_is_a = v_is_a

    row = lax.broadcasted_iota(jnp.int32, (2 * span, 2 * span), 0) % span
    col = lax.broadcasted_iota(jnp.int32, (2 * span, 2 * span), 1)
    band_scr[...] = jnp.where((col >= row) & (col <= row + span), 0.0, NEG_INF)
    row1 = lax.broadcasted_iota(jnp.int32, (2 * span, span), 0) % span
    col1 = lax.broadcasted_iota(jnp.int32, (2 * span, span), 1)
    causal_scr[...] = jnp.where(col1 <= row1, 0.0, NEG_INF)

    def scores(g, r, n, first):
        q_ref, k_ref, _ = qkv[3 * g: 3 * g + 3]
        if first:
            qb = q_ref[0, r, pl.ds(0, span), :]
            kb = k_ref[0, r, pl.ds(0, span), :]
            bias = causal_scr[...]
        else:
            k0 = pl.multiple_of((n - 1) * span, span)
            qb = q_ref[0, r, pl.ds(k0 + span, span), :]
            kb = k_ref[0, r, pl.ds(k0, 2 * span), :]
            bias = band_scr[...]
        zero = jnp.zeros_like(qb)
        q2 = jnp.concatenate([jnp.where(qk_is_a, qb, zero), jnp.where(qk_is_a, zero, qb)], axis=0)
        return _dot_nt(q2, kb) + bias

    def finish(g, dil, r, n, first, s):
        v_ref = qkv[3 * g + 2]
        if first:
            vb = v_ref[0, r, pl.ds(0, span), :]
        else:
            vb = v_ref[0, r, pl.ds(pl.multiple_of((n - 1) * span, span), 2 * span), :]
        m_a = jnp.max(s[:span], axis=-1, keepdims=True)
        m_b = jnp.max(s[span:], axis=-1, keepdims=True)
        p = jnp.concatenate([jnp.exp2(s[:span] - m_a), jnp.exp2(s[span:] - m_b)], axis=1).astype(BF16)
        zero = jnp.zeros_like(vb)
        ind_a = jnp.broadcast_to(jnp.where(v_is_a, 1.0, 0.0), vb.shape).astype(BF16)
        ind_b = jnp.broadcast_to(jnp.where(v_is_a, 0.0, 1.0), vb.shape).astype(BF16)
        rhs = jnp.concatenate([
            jnp.concatenate([jnp.where(v_is_a, vb, zero), ind_a], axis=1),
            jnp.concatenate([jnp.where(v_is_a, zero, vb), ind_b], axis=1),
        ], axis=0)
        res = _dot(p, rhs)
        l = res[:, LANES:]
        o = res[:, :LANES] * (1.0 / l)
        lse = jnp.where(v_is_a, m_a, m_b) + jnp.log2(l)
        start = n * span * dil + r
        if dil == 1:
            rows = pl.ds(pl.multiple_of(start, span), span)
        else:
            rows = pl.ds(start, span, stride=dil)
        o_scr[g, rows, :] = o
        lse_scr[g, rows, :] = lse

    def blocks(g, dil, todo):
        pending = []
        for blk in todo:
            pending.append((blk, scores(g, *blk)))
            if len(pending) > A_SCORE_LOOKAHEAD:
                (r, n, first), s = pending.pop(0)
                finish(g, dil, r, n, first, s)
        for (r, n, first), s in pending:
            finish(g, dil, r, n, first, s)

    for g, (_, dil) in enumerate(A_GROUPS):
        nb = seq // dil // span
        if dil == 1:
            blocks(g, dil, [(0, 0, True)])
            per = A_BLOCKS_PER_BODY[g]
            def body(i, carry, g=g, dil=dil, per=per):
                blocks(g, dil, [(0, 1 + i * per + j, False) for j in range(per)])
                return carry
            lax.fori_loop(0, (nb - 1) // per, body, 0)
        else:
            streams = A_BLOCKS_PER_BODY[g] // nb
            def body(i, carry, g=g, dil=dil, nb=nb, streams=streams):
                blocks(g, dil, [(i * streams + j, n, n == 0)
                                for j in range(streams) for n in range(nb)])
                return carry
            lax.fori_loop(0, dil // streams, body, 0)

    lses = [lse_scr[g] for g in range(A_N_GROUPS)]
    mx = functools.reduce(jnp.maximum, lses)
    es = [jnp.exp2(v - mx) for v in lses]
    num = functools.reduce(lambda a, b: a + b, [es[g] * o_scr[g] for g in range(A_N_GROUPS)])
    den = functools.reduce(lambda a, b: a + b, es)
    out_ref[0] = (num * (1.0 / den)).astype(BF16)


def _attn_core(qkv, seq):
    bsz = qkv[0].shape[0]
    npair = A_GROUP_COLS // LANES
    in_specs = []
    for _, dil in A_GROUPS:
        for _ in range(3):
            in_specs.append(pl.BlockSpec((1, dil, seq // dil, LANES), lambda b, p: (b, 0, 0, p)))
    return pl.pallas_call(
        functools.partial(_attn_core_kernel, seq=seq),
        grid=(bsz, npair),
        in_specs=in_specs,
        out_specs=pl.BlockSpec((1, seq, LANES), lambda b, p: (b, 0, p)),
        out_shape=jax.ShapeDtypeStruct((bsz, seq, A_GROUP_COLS), BF16),
        scratch_shapes=[pltpu.VMEM((A_N_GROUPS, seq, LANES), F32),
                        pltpu.VMEM((A_N_GROUPS, seq, LANES), F32),
                        pltpu.VMEM((2 * A_SPAN, 2 * A_SPAN), F32),
                        pltpu.VMEM((2 * A_SPAN, A_SPAN), F32)],
        compiler_params=pltpu.CompilerParams(
            dimension_semantics=("parallel", "parallel"), vmem_limit_bytes=VMEM_LIMIT),
        name="attn_core",
    )(*qkv)


def _gla_proj_kernel(x_ref, g_ref, w_ref, wgd_ref, wgu_ref, bgu_ref, tri_ref,
                     q_ref, k_ref, v_ref, r_ref, b_ref, *, tm):
    h = _rms(x_ref[...], g_ref[...]).astype(BF16)
    gd = _dot(h, wgd_ref[...]).astype(BF16)
    z = _dot(gd, wgu_ref[...]) + bgu_ref[...]
    q_ref[...] = (_dot(h, w_ref[:, :B_KEY_DIM]) * (B_DK ** -0.5)).astype(BF16)
    k_ref[...] = _dot(h, w_ref[:, B_KEY_DIM:2 * B_KEY_DIM]).astype(BF16)
    log_sig = jnp.minimum(z, 0.0) - jnp.log(1.0 + jnp.exp(-jnp.abs(z)))
    glog = log_sig * (LOG2_E / B_GATE_NORMALIZER)
    hi = glog.astype(BF16)
    rem = glog - hi.astype(F32)
    mid = rem.astype(BF16)
    lo = (rem - mid.astype(F32)).astype(BF16)
    tri = tri_ref[...]
    for s in range(tm // B_SLAB):
        rows = slice(s * B_SLAB, (s + 1) * B_SLAB)
        b_ref[rows, :] = _dot(tri, hi[rows]) + _dot(tri, mid[rows]) + _dot(tri, lo[rows])
    c0 = 2 * B_KEY_DIM
    v_ref[...] = _dot(h, w_ref[:, c0:c0 + B_VAL_DIM]).astype(BF16)
    r = _dot(h, w_ref[:, c0 + B_VAL_DIM:c0 + 2 * B_VAL_DIM])
    r_ref[...] = (r * (1.0 / (1.0 + jnp.exp2(r * -LOG2_E)))).astype(BF16)


def _gla_proj(x2, g, w_main, w_gd, w_gu, b_gu, layers, tm=512):
    t = x2.shape[0]
    li, lj = layers
    row = lambda i: (i, 0)
    idx = jnp.arange(B_SLAB)
    tri = ((idx[:, None] // B_CHUNK == idx[None, :] // B_CHUNK)
           & (idx[None, :] <= idx[:, None])).astype(BF16)
    return pl.pallas_call(
        functools.partial(_gla_proj_kernel, tm=tm),
        grid=(t // tm,),
        in_specs=[
            pl.BlockSpec((tm, D_MODEL), row),
            _resident((1, D_MODEL), li),
            _resident(w_main.shape[1:], lj),
            _resident(w_gd.shape[1:], lj),
            _resident(w_gu.shape[1:], lj),
            _resident(b_gu.shape[1:], lj),
            _resident(tri.shape),
        ],
        out_specs=[
            pl.BlockSpec((tm, B_KEY_DIM), row),
            pl.BlockSpec((tm, B_KEY_DIM), row),
            pl.BlockSpec((tm, B_VAL_DIM), row),
            pl.BlockSpec((tm, B_VAL_DIM), row),
            pl.BlockSpec((tm, B_KEY_DIM), row),
        ],
        out_shape=[
            jax.ShapeDtypeStruct((t, B_KEY_DIM), BF16),
            jax.ShapeDtypeStruct((t, B_KEY_DIM), BF16),
            jax.ShapeDtypeStruct((t, B_VAL_DIM), BF16),
            jax.ShapeDtypeStruct((t, B_VAL_DIM), BF16),
            jax.ShapeDtypeStruct((t, B_KEY_DIM), F32),
        ],
        compiler_params=pltpu.CompilerParams(
            dimension_semantics=("parallel",), vmem_limit_bytes=VMEM_LIMIT),
        name="gla_proj",
    )(x2, g, w_main, w_gd, w_gu, b_gu, tri)


def _gla_core_kernel(q_ref, k_ref, v_ref, r_ref, b_ref, gn_ref, out_ref, *, seq, unroll):
    c = B_CHUNK
    ri = lax.broadcasted_iota(jnp.int32, (c, c), 0)
    ci = lax.broadcasted_iota(jnp.int32, (c, c), 1)
    gn = gn_ref[...]

    def local(n):
        base = pl.multiple_of(n * c, c)
        rows = pl.ds(base, c)
        bb = b_ref[0, rows, :]
        b_mid = b_ref[0, pl.ds(base + c // 2 - 1, 1), :]
        b_last = b_ref[0, pl.ds(base + c - 1, 1), :]
        qf = q_ref[0, rows, :].astype(F32)
        kf = k_ref[0, rows, :].astype(F32)
        vb = v_ref[0, rows, :]
        x_mid = bb - b_mid
        qe = (qf * jnp.exp2(x_mid)).astype(BF16)
        ke = (kf * jnp.exp2(-x_mid)).astype(BF16)
        a = jnp.where(ci <= ri, _dot_nt(qe, ke), 0.0).astype(BF16)
        qi = (qf * jnp.exp2(bb)).astype(BF16)
        ks = (kf * jnp.exp2(b_last - bb)).astype(BF16)
        return rows, a, vb, qi, _dot_tn(vb, ks), jnp.exp2(b_last)

    def finish(st, rows, a, vb, qi, kv, decay):
        o = _dot(a, vb) + _dot_nt(qi, st.astype(BF16))
        o = _rms(o, gn)
        out_ref[0, rows, :] = (o * r_ref[0, rows, :].astype(F32)).astype(BF16)
        return st * decay + kv

    def body(i, st):
        pending = []
        for j in range(unroll):
            pending.append(local(i * unroll + j))
            if len(pending) > B_LOCAL_LOOKAHEAD:
                st = finish(st, *pending.pop(0))
        for part in pending:
            st = finish(st, *part)
        return st

    lax.fori_loop(0, seq // c // unroll, body, jnp.zeros((B_DV, B_DK), F32))


def _gla_core(q, k, v, r, b, g_norm, layer, seq, unroll=32):
    bsz = q.shape[0]
    hblock = lambda width: pl.BlockSpec((1, seq, width), lambda b, h: (b, 0, h))
    return pl.pallas_call(
        functools.partial(_gla_core_kernel, seq=seq, unroll=unroll),
        grid=(bsz, B_HEADS),
        in_specs=[hblock(B_DK), hblock(B_DK), hblock(B_DV), hblock(B_DV), hblock(B_DK),
                  _resident((1, B_DV), layer)],
        out_specs=hblock(B_DV),
        out_shape=jax.ShapeDtypeStruct((bsz, seq, B_VAL_DIM), BF16),
        compiler_params=pltpu.CompilerParams(
            dimension_semantics=("parallel", "parallel"), vmem_limit_bytes=VMEM_LIMIT),
        name="gla_core",
    )(q, k, v, r, b, g_norm)


def _ffn_kernel(x_ref, mix_ref, wo_ref, g_ref, win_ref, cw_ref, cb_ref, wd_ref, gf_ref,
                out_ref, carry_scr, act_scr, *, tm, nsub, final):
    i = pl.program_id(1)

    @pl.when(i == 0)
    def _():
        carry_scr[...] = jnp.zeros_like(carry_scr)

    ts = tm // nsub
    subs = [slice(s * ts, (s + 1) * ts) for s in range(nsub)]
    x1 = [x_ref[0, rows, :] + _dot(mix_ref[0, rows, :], wo_ref[...]) for rows in subs]
    h = [_rms(v, g_ref[...]).astype(BF16) for v in x1]
    row = lax.broadcasted_iota(jnp.int32, (ts, FFN_CHUNK), 0)
    for s, rows in enumerate(subs):
        for c in range(FFN_NCHUNK):
            cols = slice(c * FFN_CHUNK, (c + 1) * FFN_CHUNK)
            a = _dot(h[s], win_ref[:, cols])
            u = _dot(h[s], win_ref[:, FFN_DIM + c * FFN_CHUNK:FFN_DIM + (c + 1) * FFN_CHUNK])
            prev2 = carry_scr[6:7, cols]
            prev1 = carry_scr[7:8, cols]
            a1 = jnp.where(row == 0, prev1, pltpu.roll(a, 1, axis=0))
            a2 = jnp.where(row == 0, prev2, jnp.where(row == 1, prev1, pltpu.roll(a, 2, axis=0)))
            carry_scr[:, cols] = a[ts - 8:, :]
            conv = (a * cw_ref[2:3, cols] + a1 * cw_ref[1:2, cols] + a2 * cw_ref[0:1, cols]
                    + cb_ref[:, cols])
            act = conv * (1.0 / (1.0 + jnp.exp2(conv * -LOG2_E))) * u
            act_scr[rows, cols] = act.astype(BF16)
        y = x1[s] + _dot(act_scr[rows, :], wd_ref[...])
        if final:
            y = _rms(y, gf_ref[...])
        out_ref[0, rows, :] = y


def _ffn(x, mix, w_o, g, w_in, conv_w, conv_b, w_down, g_final, layers, final, tm=1024, nsub=2):
    bsz, seq, _ = x.shape
    kmix = mix.shape[-1]
    li, lj = layers
    return pl.pallas_call(
        functools.partial(_ffn_kernel, tm=tm, nsub=nsub, final=final),
        grid=(bsz, seq // tm),
        in_specs=[
            pl.BlockSpec((1, tm, D_MODEL), lambda b, i: (b, i, 0)),
            pl.BlockSpec((1, tm, kmix), lambda b, i: (b, i, 0)),
            _resident(w_o.shape[1:], lj),
            _resident((1, D_MODEL), li),
            _resident(w_in.shape[1:], li),
            _resident(conv_w.shape[1:], li),
            _resident(conv_b.shape[1:], li),
            _resident(w_down.shape[1:], li),
            _resident((1, D_MODEL)),
        ],
        out_specs=pl.BlockSpec((1, tm, D_MODEL), lambda b, i: (b, i, 0)),
        out_shape=jax.ShapeDtypeStruct(x.shape, F32),
        scratch_shapes=[pltpu.VMEM((8, FFN_DIM), F32), pltpu.VMEM((tm, FFN_DIM), BF16)],
        compiler_params=pltpu.CompilerParams(
            dimension_semantics=("arbitrary", "arbitrary"), vmem_limit_bytes=VMEM_LIMIT),
        name="ffn",
    )(x, mix, w_o, g, w_in, conv_w, conv_b, w_down, g_final)


def kernel(x, positions, norm_mix, norm_ffn, a_w_qkv, a_w_o, b_w_in, b_w_gate_up,
           b_b_gate_up, b_g_norm, b_w_o, f_w_in, f_conv_w, f_conv_b, f_w_down, norm_final):
    bsz, seq, _ = x.shape
    depth = norm_mix.shape[0]
    g_mix = norm_mix[:, None, :]
    g_ffn = norm_ffn[:, None, :]
    g_final = norm_final[None, :]
    a_w = a_w_qkv.astype(BF16)
    a_wo = a_w_o.astype(BF16)
    ncore = 2 * B_KEY_DIM + 2 * B_VAL_DIM
    b_w = b_w_in.astype(BF16)
    b_wgd = jnp.pad(b_w_in[:, :, ncore:], ((0, 0), (0, 0), (0, LANES - B_GATE_RANK))).astype(BF16)
    b_wgu = jnp.pad(b_w_gate_up, ((0, 0), (0, LANES - B_GATE_RANK), (0, 0))).astype(BF16)
    b_bgu = b_b_gate_up[:, None, :]
    b_gn = b_g_norm[:, None, :]
    b_wo = b_w_o.astype(BF16)
    f_win = f_w_in.astype(BF16)
    f_wd = f_w_down.astype(BF16)
    f_cb = f_conv_b[:, None, :]

    tables = _rope_tables(positions)
    for i in range(depth):
        j = i // 2
        if i % 2 == 0:
            qkv = _attn_proj(x, g_mix, a_w, (i, j), tables)
            mix = _attn_core(qkv, seq)
            w_o = a_wo
        else:
            q, k, v, r, b = _gla_proj(x.reshape(bsz * seq, D_MODEL), g_mix, b_w, b_wgd, b_wgu,
                                      b_bgu, (i, j))
            shp = lambda t: t.reshape(bsz, seq, t.shape[-1])
            mix = _gla_core(shp(q), shp(k), shp(v), shp(r), shp(b), b_gn, j, seq)
            w_o = b_wo
        x = _ffn(x, mix, w_o, g_ffn, f_win, f_conv_w, f_cb, f_wd, g_final, (i, j),
                 final=(i == depth - 1))
    return x
```

```python
import functools

import jax
import jax.numpy as jnp
from jax import lax
from jax.experimental import pallas as pl
from jax.experimental.pallas import tpu as pltpu

D_MODEL = 1024
RMS_EPS = 1e-6

A_GROUPS = ((128, 1), (512, 4), (2048, 16))
A_N_GROUPS = len(A_GROUPS)
A_HEADS_PER_GROUP = 8
A_HEAD_DIM = 64
A_GROUP_COLS = A_HEADS_PER_GROUP * A_HEAD_DIM
A_SPAN = 128
A_BLOCKS_PER_BODY = (15, 16, 16)
A_SCORE_LOOKAHEAD = 3
ROPE_THETA = 10000.0
NEG_INF = -1e30
LOG2_E = 1.4426950408889634

B_HEADS = 4
B_KEY_DIM = 512
B_VAL_DIM = 1024
B_DK = B_KEY_DIM // B_HEADS
B_DV = B_VAL_DIM // B_HEADS
B_GATE_RANK = 16
B_GATE_NORMALIZER = 16.0
B_CHUNK = 64
B_SLAB = 256
B_LOCAL_LOOKAHEAD = 2

FFN_DIM = 2816
FFN_CHUNK = 256
FFN_NCHUNK = FFN_DIM // FFN_CHUNK
CONV_WIDTH = 3

LANES = 128
ROPE_PACK = LANES // (A_HEAD_DIM // 2)
VMEM_LIMIT = 52 * 1024 * 1024

BF16 = jnp.bfloat16
F32 = jnp.float32


def _resident(shape, layer=None):
    nd = len(shape)
    if layer is None:
        return pl.BlockSpec(shape, lambda *_: (0,) * nd, pipeline_mode=pl.Buffered(1))
    return pl.BlockSpec((None,) + tuple(shape), lambda *_: (layer,) + (0,) * nd,
                        pipeline_mode=pl.Buffered(1))


def _rms(x, g):
    ms = jnp.mean(x * x, axis=-1, keepdims=True)
    return x * lax.rsqrt(ms + RMS_EPS) * g


def _dot(a, b):
    return jnp.dot(a, b, preferred_element_type=F32)


def _dot_nt(a, b):
    return lax.dot_general(a, b, (((1,), (1,)), ((), ())), preferred_element_type=F32)


def _dot_tn(a, b):
    return lax.dot_general(a, b, (((0,), (0,)), ((), ())), preferred_element_type=F32)


def _rope_table_kernel(pos_ref, inv_ref, cos_ref, sin_lo_ref, sin_hi_ref, *, tq):
    half = A_HEAD_DIM // 2
    ang = pos_ref[...].astype(F32) * inv_ref[...]
    lane = lax.broadcasted_iota(jnp.int32, ang.shape, 1)
    group = lane // half
    is_lo = (lane % A_HEAD_DIM) < half
    for src, dsts in ((jnp.cos(ang), (cos_ref,)), (jnp.sin(ang), (sin_lo_ref, sin_hi_ref))):
        rolled = [src] + [pltpu.roll(src, half * k, axis=1) for k in range(1, ROPE_PACK)]
        for c in range(ROPE_PACK):
            val = rolled[(ROPE_PACK - 1 - c) % ROPE_PACK]
            for grp in range(ROPE_PACK - 2, -1, -1):
                val = jnp.where(group == grp, rolled[(grp - c) % ROPE_PACK], val)
            rows = pl.ds(c, tq, stride=ROPE_PACK)
            if len(dsts) == 1:
                dsts[0][rows, :] = val
            else:
                dsts[0][rows, :] = jnp.where(is_lo, -val, 0.0)
                dsts[1][rows, :] = jnp.where(is_lo, 0.0, val)


def _rope_tables(positions):
    t = positions.size
    tm = min(t, 2048)
    tq = tm // ROPE_PACK
    half = A_HEAD_DIM // 2
    inv = ROPE_THETA ** (-jnp.arange(0, A_HEAD_DIM, 2, dtype=F32) / A_HEAD_DIM)
    inv = jnp.tile(inv, ROPE_PACK)[None, :]
    pos = jnp.repeat(positions.reshape(t // ROPE_PACK, ROPE_PACK), half, axis=1)
    return pl.pallas_call(
        functools.partial(_rope_table_kernel, tq=tq),
        grid=(t // tm,),
        in_specs=[
            pl.BlockSpec((tq, LANES), lambda i: (i, 0)),
            pl.BlockSpec((1, LANES), lambda i: (0, 0)),
        ],
        out_specs=[pl.BlockSpec((tm, LANES), lambda i: (i, 0))] * 3,
        out_shape=[jax.ShapeDtypeStruct((t, LANES), F32)] * 3,
        name="rope_tables",
    )(pos, inv)


def _attn_proj_kernel(x_ref, g_ref, w_ref, cos_ref, sin_lo_ref, sin_hi_ref, *refs, tm):
    outs = refs[: 3 * A_N_GROUPS]
    scr = refs[3 * A_N_GROUPS]
    h = _rms(x_ref[0], g_ref[...]).astype(BF16)
    cos = cos_ref[...]
    sin_lo = sin_lo_ref[...]
    sin_hi = sin_hi_ref[...]
    half = A_HEAD_DIM // 2

    def project(part, g):
        c0 = (part * A_N_GROUPS + g) * A_GROUP_COLS
        return _dot(h, w_ref[:, c0:c0 + A_GROUP_COLS])

    def emit(part, g, y, slot):
        dil = A_GROUPS[g][1]
        o_ref = outs[g * 3 + part]
        for j in range(A_GROUP_COLS // LANES):
            lanes = slice(j * LANES, (j + 1) * LANES)
            yb = y[:, lanes]
            if part < 2:
                yb = (yb * cos + pltpu.roll(yb, LANES - half, axis=1) * sin_lo
                      + pltpu.roll(yb, half, axis=1) * sin_hi)
                if part == 0:
                    yb = yb * (A_HEAD_DIM ** -0.5 * LOG2_E)
            if dil == 1:
                o_ref[0, 0, :, lanes] = yb.astype(BF16)
            else:
                scr[slot, j] = yb
                for r in range(dil):
                    o_ref[0, r, :, lanes] = scr[slot, j, pl.ds(r, tm // dil, stride=dil), :].astype(BF16)

    order = [(0, 2), (1, 2), (0, 1), (1, 1), (2, 2), (2, 1), (0, 0), (1, 0), (2, 0)]
    y_next = project(*order[0])
    for idx, (part, g) in enumerate(order):
        y = y_next
        if idx + 1 < len(order):
            y_next = project(*order[idx + 1])
        emit(part, g, y, idx % 2)


def _attn_proj(x, g, w, layer, tables, tm=1024):
    bsz, seq, _ = x.shape
    nt = seq // tm
    out_shapes, out_specs = [], []
    for _, dil in A_GROUPS:
        for _ in range(3):
            out_shapes.append(jax.ShapeDtypeStruct((bsz, dil, seq // dil, A_GROUP_COLS), BF16))
            out_specs.append(pl.BlockSpec((1, dil, tm // dil, A_GROUP_COLS), lambda b, i: (b, 0, i, 0)))
    return pl.pallas_call(
        functools.partial(_attn_proj_kernel, tm=tm),
        grid=(bsz, nt),
        in_specs=[
            pl.BlockSpec((1, tm, D_MODEL), lambda b, i: (b, i, 0)),
            _resident((1, D_MODEL), layer[0]),
            _resident(w.shape[1:], layer[1]),
        ] + [pl.BlockSpec((tm, LANES), lambda b, i: (b * nt + i, 0))] * len(tables),
        out_specs=out_specs,
        out_shape=out_shapes,
        scratch_shapes=[pltpu.VMEM((2, A_GROUP_COLS // LANES, tm, LANES), F32)],
        compiler_params=pltpu.CompilerParams(
            dimension_semantics=("parallel", "parallel"), vmem_limit_bytes=VMEM_LIMIT),
        name="attn_proj",
    )(x, g, w, *tables)


def _attn_core_kernel(*refs, seq):
    qkv = refs[: 3 * A_N_GROUPS]
    out_ref = refs[3 * A_N_GROUPS]
    o_scr, lse_scr, band_scr, causal_scr = refs[3 * A_N_GROUPS + 1:]
    span = A_SPAN
    lane = lax.broadcasted_iota(jnp.int32, (1, LANES), 1)
    v_is_a = lane < (LANES // 2)
    qk_is_a = v_is_a

    row = lax.broadcasted_iota(jnp.int32, (2 * span, 2 * span), 0) % span
    col = lax.broadcasted_iota(jnp.int32, (2 * span, 2 * span), 1)
    band_scr[...] = jnp.where((col >= row) & (col <= row + span), 0.0, NEG_INF)
    row1 = lax.broadcasted_iota(jnp.int32, (2 * span, span), 0) % span
    col1 = lax.broadcasted_iota(jnp.int32, (2 * span, span), 1)
    causal_scr[...] = jnp.where(col1 <= row1, 0.0, NEG_INF)

    def scores(g, r, n, first):
        q_ref, k_ref, _ = qkv[3 * g: 3 * g + 3]
        if first:
            qb = q_ref[0, r, pl.ds(0, span), :]
            kb = k_ref[0, r, pl.ds(0, span), :]
            bias = causal_scr[...]
        else:
            k0 = pl.multiple_of((n - 1) * span, span)
            qb = q_ref[0, r, pl.ds(k0 + span, span), :]
            kb = k_ref[0, r, pl.ds(k0, 2 * span), :]
            bias = band_scr[...]
        zero = jnp.zeros_like(qb)
        q2 = jnp.concatenate([jnp.where(qk_is_a, qb, zero), jnp.where(qk_is_a, zero, qb)], axis=0)
        return _dot_nt(q2, kb) + bias

    def finish(g, dil, r, n, first, s):
        v_ref = qkv[3 * g + 2]
        if first:
            vb = v_ref[0, r, pl.ds(0, span), :]
        else:
            vb = v_ref[0, r, pl.ds(pl.multiple_of((n - 1) * span, span), 2 * span), :]
        m_a = jnp.max(s[:span], axis=-1, keepdims=True)
        m_b = jnp.max(s[span:], axis=-1, keepdims=True)
        p = jnp.concatenate([jnp.exp2(s[:span] - m_a), jnp.exp2(s[span:] - m_b)], axis=1).astype(BF16)
        zero = jnp.zeros_like(vb)
        ind_a = jnp.broadcast_to(jnp.where(v_is_a, 1.0, 0.0), vb.shape).astype(BF16)
        ind_b = jnp.broadcast_to(jnp.where(v_is_a, 0.0, 1.0), vb.shape).astype(BF16)
        rhs = jnp.concatenate([
            jnp.concatenate([jnp.where(v_is_a, vb, zero), ind_a], axis=1),
            jnp.concatenate([jnp.where(v_is_a, zero, vb), ind_b], axis=1),
        ], axis=0)
        res = _dot(p, rhs)
        l = res[:, LANES:]
        o = res[:, :LANES] * (1.0 / l)
        lse = jnp.where(v_is_a, m_a, m_b) + jnp.log2(l)
        start = n * span * dil + r
        if dil == 1:
            rows = pl.ds(pl.multiple_of(start, span), span)
        else:
            rows = pl.ds(start, span, stride=dil)
        o_scr[g, rows, :] = o
        lse_scr[g, rows, :] = lse

    def blocks(g, dil, todo):
        pending = []
        for blk in todo:
            pending.append((blk, scores(g, *blk)))
            if len(pending) > A_SCORE_LOOKAHEAD:
                (r, n, first), s = pending.pop(0)
                finish(g, dil, r, n, first, s)
        for (r, n, first), s in pending:
            finish(g, dil, r, n, first, s)

    for g, (_, dil) in enumerate(A_GROUPS):
        nb = seq // dil // span
        if dil == 1:
            blocks(g, dil, [(0, 0, True)])
            per = A_BLOCKS_PER_BODY[g]
            def body(i, carry, g=g, dil=dil, per=per):
                blocks(g, dil, [(0, 1 + i * per + j, False) for j in range(per)])
                return carry
            lax.fori_loop(0, (nb - 1) // per, body, 0)
        else:
            streams = A_BLOCKS_PER_BODY[g] // nb
            def body(i, carry, g=g, dil=dil, nb=nb, streams=streams):
                blocks(g, dil, [(i * streams + j, n, n == 0)
                                for j in range(streams) for n in range(nb)])
                return carry
            lax.fori_loop(0, dil // streams, body, 0)

    lses = [lse_scr[g] for g in range(A_N_GROUPS)]
    mx = functools.reduce(jnp.maximum, lses)
    es = [jnp.exp2(v - mx) for v in lses]
    num = functools.reduce(lambda a, b: a + b, [es[g] * o_scr[g] for g in range(A_N_GROUPS)])
    den = functools.reduce(lambda a, b: a + b, es)
    out_ref[0] = (num * (1.0 / den)).astype(BF16)


def _attn_core(qkv, seq):
    bsz = qkv[0].shape[0]
    npair = A_GROUP_COLS // LANES
    in_specs = []
    for _, dil in A_GROUPS:
        for _ in range(3):
            in_specs.append(pl.BlockSpec((1, dil, seq // dil, LANES), lambda b, p: (b, 0, 0, p)))
    return pl.pallas_call(
        functools.partial(_attn_core_kernel, seq=seq),
        grid=(bsz, npair),
        in_specs=in_specs,
        out_specs=pl.BlockSpec((1, seq, LANES), lambda b, p: (b, 0, p)),
        out_shape=jax.ShapeDtypeStruct((bsz, seq, A_GROUP_COLS), BF16),
        scratch_shapes=[pltpu.VMEM((A_N_GROUPS, seq, LANES), F32),
                        pltpu.VMEM((A_N_GROUPS, seq, LANES), F32),
                        pltpu.VMEM((2 * A_SPAN, 2 * A_SPAN), F32),
                        pltpu.VMEM((2 * A_SPAN, A_SPAN), F32)],
        compiler_params=pltpu.CompilerParams(
            dimension_semantics=("parallel", "parallel"), vmem_limit_bytes=VMEM_LIMIT),
        name="attn_core",
    )(*qkv)


def _gla_proj_kernel(x_ref, g_ref, w_ref, wgd_ref, wgu_ref, bgu_ref, tri_ref,
                     q_ref, k_ref, v_ref, r_ref, b_ref, *, tm):
    h = _rms(x_ref[...], g_ref[...]).astype(BF16)
    gd = _dot(h, wgd_ref[...]).astype(BF16)
    z = _dot(gd, wgu_ref[...]) + bgu_ref[...]
    q_ref[...] = (_dot(h, w_ref[:, :B_KEY_DIM]) * (B_DK ** -0.5)).astype(BF16)
    k_ref[...] = _dot(h, w_ref[:, B_KEY_DIM:2 * B_KEY_DIM]).astype(BF16)
    log_sig = jnp.minimum(z, 0.0) - jnp.log(1.0 + jnp.exp(-jnp.abs(z)))
    glog = log_sig * (LOG2_E / B_GATE_NORMALIZER)
    hi = glog.astype(BF16)
    rem = glog - hi.astype(F32)
    mid = rem.astype(BF16)
    lo = (rem - mid.astype(F32)).astype(BF16)
    tri = tri_ref[...]
    for s in range(tm // B_SLAB):
        rows = slice(s * B_SLAB, (s + 1) * B_SLAB)
        b_ref[rows, :] = _dot(tri, hi[rows]) + _dot(tri, mid[rows]) + _dot(tri, lo[rows])
    c0 = 2 * B_KEY_DIM
    v_ref[...] = _dot(h, w_ref[:, c0:c0 + B_VAL_DIM]).astype(BF16)
    r = _dot(h, w_ref[:, c0 + B_VAL_DIM:c0 + 2 * B_VAL_DIM])
    r_ref[...] = (r * (1.0 / (1.0 + jnp.exp2(r * -LOG2_E)))).astype(BF16)


def _gla_proj(x2, g, w_main, w_gd, w_gu, b_gu, layers, tm=512):
    t = x2.shape[0]
    li, lj = layers
    row = lambda i: (i, 0)
    idx = jnp.arange(B_SLAB)
    tri = ((idx[:, None] // B_CHUNK == idx[None, :] // B_CHUNK)
           & (idx[None, :] <= idx[:, None])).astype(BF16)
    return pl.pallas_call(
        functools.partial(_gla_proj_kernel, tm=tm),
        grid=(t // tm,),
        in_specs=[
            pl.BlockSpec((tm, D_MODEL), row),
            _resident((1, D_MODEL), li),
            _resident(w_main.shape[1:], lj),
            _resident(w_gd.shape[1:], lj),
            _resident(w_gu.shape[1:], lj),
            _resident(b_gu.shape[1:], lj),
            _resident(tri.shape),
        ],
        out_specs=[
            pl.BlockSpec((tm, B_KEY_DIM), row),
            pl.BlockSpec((tm, B_KEY_DIM), row),
            pl.BlockSpec((tm, B_VAL_DIM), row),
            pl.BlockSpec((tm, B_VAL_DIM), row),
            pl.BlockSpec((tm, B_KEY_DIM), row),
        ],
        out_shape=[
            jax.ShapeDtypeStruct((t, B_KEY_DIM), BF16),
            jax.ShapeDtypeStruct((t, B_KEY_DIM), BF16),
            jax.ShapeDtypeStruct((t, B_VAL_DIM), BF16),
            jax.ShapeDtypeStruct((t, B_VAL_DIM), BF16),
            jax.ShapeDtypeStruct((t, B_KEY_DIM), F32),
        ],
        compiler_params=pltpu.CompilerParams(
            dimension_semantics=("parallel",), vmem_limit_bytes=VMEM_LIMIT),
        name="gla_proj",
    )(x2, g, w_main, w_gd, w_gu, b_gu, tri)


def _gla_core_kernel(q_ref, k_ref, v_ref, r_ref, b_ref, gn_ref, out_ref, *, seq, unroll, heads):
    c = B_CHUNK
    ri = lax.broadcasted_iota(jnp.int32, (c, c), 0)
    ci = lax.broadcasted_iota(jnp.int32, (c, c), 1)
    gn = gn_ref[...]

    def local(n, hh):
        base = n * c if isinstance(n, int) else pl.multiple_of(n * c, c)
        rows = pl.ds(base, c)
        kl = slice(hh * B_DK, (hh + 1) * B_DK)
        vl = slice(hh * B_DV, (hh + 1) * B_DV)
        bb = b_ref[0, rows, kl]
        b_mid = b_ref[0, pl.ds(base + c // 2 - 1, 1), kl]
        b_last = b_ref[0, pl.ds(base + c - 1, 1), kl]
        qf = q_ref[0, rows, kl].astype(F32)
        kf = k_ref[0, rows, kl].astype(F32)
        vb = v_ref[0, rows, vl]
        x_mid = bb - b_mid
        qe = (qf * jnp.exp2(x_mid)).astype(BF16)
        ke = (kf * jnp.exp2(-x_mid)).astype(BF16)
        a = jnp.where(ci <= ri, _dot_nt(qe, ke), 0.0).astype(BF16)
        qi = (qf * jnp.exp2(bb)).astype(BF16)
        ks = (kf * jnp.exp2(b_last - bb)).astype(BF16)
        return rows, vl, a, vb, qi, _dot_tn(vb, ks), jnp.exp2(b_last)

    def finish(st, rows, vl, a, vb, qi, kv, decay):
        o = _dot(a, vb) + _dot_nt(qi, st.astype(BF16))
        o = _rms(o, gn)
        out_ref[0, rows, vl] = (o * r_ref[0, rows, vl].astype(F32)).astype(BF16)
        return st * decay + kv

    def body(i, sts):
        sts = list(sts)
        pending = []
        for j in range(unroll):
            for hh in range(heads):
                pending.append((hh, local(i * unroll + j, hh)))
                if len(pending) > B_LOCAL_LOOKAHEAD * heads:
                    hq, part = pending.pop(0)
                    sts[hq] = finish(sts[hq], *part)
        for hq, part in pending:
            sts[hq] = finish(sts[hq], *part)
        return tuple(sts)

    init = tuple(jnp.zeros((B_DV, B_DK), F32) for _ in range(heads))
    trips = seq // c // unroll
    if trips == 1:
        body(0, init)
    else:
        lax.fori_loop(0, trips, body, init)


def _gla_core(q, k, v, r, b, g_norm, layer, seq, unroll=32, heads=2):
    bsz = q.shape[0]
    hblock = lambda width: pl.BlockSpec((1, seq, heads * width), lambda b, h: (b, 0, h))
    return pl.pallas_call(
        functools.partial(_gla_core_kernel, seq=seq, unroll=unroll, heads=heads),
        grid=(bsz, B_HEADS // heads),
        in_specs=[hblock(B_DK), hblock(B_DK), hblock(B_DV), hblock(B_DV), hblock(B_DK),
                  _resident((1, B_DV), layer)],
        out_specs=hblock(B_DV),
        out_shape=jax.ShapeDtypeStruct((bsz, seq, B_VAL_DIM), BF16),
        compiler_params=pltpu.CompilerParams(
            dimension_semantics=("parallel", "parallel"), vmem_limit_bytes=VMEM_LIMIT),
        name="gla_core",
    )(q, k, v, r, b, g_norm)


def _ffn_kernel(x_ref, mix_ref, wo_ref, g_ref, win_ref, cw_ref, cb_ref, wd_ref, gf_ref,
                out_ref, carry_scr, act_scr, *, tm, nsub, final):
    i = pl.program_id(1)

    @pl.when(i == 0)
    def _():
        carry_scr[...] = jnp.zeros_like(carry_scr)

    ts = tm // nsub
    subs = [slice(s * ts, (s + 1) * ts) for s in range(nsub)]
    x1 = [x_ref[0, rows, :] + _dot(mix_ref[0, rows, :], wo_ref[...]) for rows in subs]
    h = [_rms(v, g_ref[...]).astype(BF16) for v in x1]
    row = lax.broadcasted_iota(jnp.int32, (ts, FFN_CHUNK), 0)
    for s, rows in enumerate(subs):
        for c in range(FFN_NCHUNK):
            cols = slice(c * FFN_CHUNK, (c + 1) * FFN_CHUNK)
            a = _dot(h[s], win_ref[:, cols])
            u = _dot(h[s], win_ref[:, FFN_DIM + c * FFN_CHUNK:FFN_DIM + (c + 1) * FFN_CHUNK])
            prev2 = carry_scr[6:7, cols]
            prev1 = carry_scr[7:8, cols]
            a1 = jnp.where(row == 0, prev1, pltpu.roll(a, 1, axis=0))
            a2 = jnp.where(row == 0, prev2, jnp.where(row == 1, prev1, pltpu.roll(a, 2, axis=0)))
            carry_scr[:, cols] = a[ts - 8:, :]
            conv = (a * cw_ref[2:3, cols] + a1 * cw_ref[1:2, cols] + a2 * cw_ref[0:1, cols]
                    + cb_ref[:, cols])
            act = conv * (1.0 / (1.0 + jnp.exp2(conv * -LOG2_E))) * u
            act_scr[rows, cols] = act.astype(BF16)
        y = x1[s] + _dot(act_scr[rows, :], wd_ref[...])
        if final:
            y = _rms(y, gf_ref[...])
        out_ref[0, rows, :] = y


def _ffn(x, mix, w_o, g, w_in, conv_w, conv_b, w_down, g_final, layers, final, tm=1024, nsub=2):
    bsz, seq, _ = x.shape
    kmix = mix.shape[-1]
    li, lj = layers
    return pl.pallas_call(
        functools.partial(_ffn_kernel, tm=tm, nsub=nsub, final=final),
        grid=(bsz, seq // tm),
        in_specs=[
            pl.BlockSpec((1, tm, D_MODEL), lambda b, i: (b, i, 0)),
            pl.BlockSpec((1, tm, kmix), lambda b, i: (b, i, 0)),
            _resident(w_o.shape[1:], lj),
            _resident((1, D_MODEL), li),
            _resident(w_in.shape[1:], li),
            _resident(conv_w.shape[1:], li),
            _resident(conv_b.shape[1:], li),
            _resident(w_down.shape[1:], li),
            _resident((1, D_MODEL)),
        ],
        out_specs=pl.BlockSpec((1, tm, D_MODEL), lambda b, i: (b, i, 0)),
        out_shape=jax.ShapeDtypeStruct(x.shape, F32),
        scratch_shapes=[pltpu.VMEM((8, FFN_DIM), F32), pltpu.VMEM((tm, FFN_DIM), BF16)],
        compiler_params=pltpu.CompilerParams(
            dimension_semantics=("arbitrary", "arbitrary"), vmem_limit_bytes=VMEM_LIMIT),
        name="ffn",
    )(x, mix, w_o, g, w_in, conv_w, conv_b, w_down, g_final)


def kernel(x, positions, norm_mix, norm_ffn, a_w_qkv, a_w_o, b_w_in, b_w_gate_up,
           b_b_gate_up, b_g_norm, b_w_o, f_w_in, f_conv_w, f_conv_b, f_w_down, norm_final):
    bsz, seq, _ = x.shape
    depth = norm_mix.shape[0]
    g_mix = norm_mix[:, None, :]
    g_ffn = norm_ffn[:, None, :]
    g_final = norm_final[None, :]
    a_w = a_w_qkv.astype(BF16)
    a_wo = a_w_o.astype(BF16)
    ncore = 2 * B_KEY_DIM + 2 * B_VAL_DIM
    b_w = b_w_in[:, :, :ncore].astype(BF16)
    b_wgd = jnp.pad(b_w_in[:, :, ncore:], ((0, 0), (0, 0), (0, LANES - B_GATE_RANK))).astype(BF16)
    b_wgu = jnp.pad(b_w_gate_up, ((0, 0), (0, LANES - B_GATE_RANK), (0, 0))).astype(BF16)
    b_bgu = b_b_gate_up[:, None, :]
    b_gn = b_g_norm[:, None, :]
    b_wo = b_w_o.astype(BF16)
    f_win = f_w_in.astype(BF16)
    f_wd = f_w_down.astype(BF16)
    f_cb = f_conv_b[:, None, :]

    tables = _rope_tables(positions)
    for i in range(depth):
        j = i // 2
        if i % 2 == 0:
            qkv = _attn_proj(x, g_mix, a_w, (i, j), tables)
            mix = _attn_core(qkv, seq)
            w_o = a_wo
        else:
            q, k, v, r, b = _gla_proj(x.reshape(bsz * seq, D_MODEL), g_mix, b_w, b_wgd, b_wgu,
                                      b_bgu, (i, j))
            shp = lambda t: t.reshape(bsz, seq, t.shape[-1])
            mix = _gla_core(shp(q), shp(k), shp(v), shp(r), shp(b), b_gn, j, seq)
            w_o = b_wo
        x = _ffn(x, mix, w_o, g_ffn, f_win, f_conv_w, f_cb, f_wd, g_final, (i, j),
                 final=(i == depth - 1))
    return x
```

```python
import functools

import jax
import jax.numpy as jnp
from jax import lax
from jax.experimental import pallas as pl
from jax.experimental.pallas import tpu as pltpu

D_MODEL = 1024
RMS_EPS = 1e-6

A_GROUPS = ((128, 1), (512, 4), (2048, 16))
A_N_GROUPS = len(A_GROUPS)
A_HEADS_PER_GROUP = 8
A_HEAD_DIM = 64
A_GROUP_COLS = A_HEADS_PER_GROUP * A_HEAD_DIM
A_SPAN = 128
A_BLOCKS_PER_BODY = (15, 16, 16)
A_SCORE_LOOKAHEAD = 3
ROPE_THETA = 10000.0
NEG_INF = -1e30
LOG2_E = 1.4426950408889634

B_HEADS = 4
B_KEY_DIM = 512
B_VAL_DIM = 1024
B_DK = B_KEY_DIM // B_HEADS
B_DV = B_VAL_DIM // B_HEADS
B_GATE_RANK = 16
B_GATE_NORMALIZER = 16.0
B_CHUNK = 64
B_SLAB = 256
B_LOCAL_LOOKAHEAD = 2

FFN_DIM = 2816
FFN_CHUNK = 256
FFN_NCHUNK = FFN_DIM // FFN_CHUNK
CONV_WIDTH = 3

LANES = 128
ROPE_PACK = LANES // (A_HEAD_DIM // 2)
VMEM_LIMIT = 52 * 1024 * 1024

BF16 = jnp.bfloat16
F32 = jnp.float32


def _resident(shape, layer=None):
    nd = len(shape)
    if layer is None:
        return pl.BlockSpec(shape, lambda *_: (0,) * nd, pipeline_mode=pl.Buffered(1))
    return pl.BlockSpec((None,) + tuple(shape), lambda *_: (layer,) + (0,) * nd,
                        pipeline_mode=pl.Buffered(1))


def _rms(x, g):
    ms = jnp.mean(x * x, axis=-1, keepdims=True)
    return x * lax.rsqrt(ms + RMS_EPS) * g


def _dot(a, b):
    return jnp.dot(a, b, preferred_element_type=F32)


def _dot_nt(a, b):
    return lax.dot_general(a, b, (((1,), (1,)), ((), ())), preferred_element_type=F32)


def _dot_tn(a, b):
    return lax.dot_general(a, b, (((0,), (0,)), ((), ())), preferred_element_type=F32)


def _rope_table_kernel(pos_ref, inv_ref, cos_ref, sin_lo_ref, sin_hi_ref, *, tq):
    half = A_HEAD_DIM // 2
    ang = pos_ref[...].astype(F32) * inv_ref[...]
    lane = lax.broadcasted_iota(jnp.int32, ang.shape, 1)
    group = lane // half
    is_lo = (lane % A_HEAD_DIM) < half
    for src, dsts in ((jnp.cos(ang), (cos_ref,)), (jnp.sin(ang), (sin_lo_ref, sin_hi_ref))):
        rolled = [src] + [pltpu.roll(src, half * k, axis=1) for k in range(1, ROPE_PACK)]
        for c in range(ROPE_PACK):
            val = rolled[(ROPE_PACK - 1 - c) % ROPE_PACK]
            for grp in range(ROPE_PACK - 2, -1, -1):
                val = jnp.where(group == grp, rolled[(grp - c) % ROPE_PACK], val)
            rows = pl.ds(c, tq, stride=ROPE_PACK)
            if len(dsts) == 1:
                dsts[0][rows, :] = val
            else:
                dsts[0][rows, :] = jnp.where(is_lo, -val, 0.0)
                dsts[1][rows, :] = jnp.where(is_lo, 0.0, val)


def _rope_tables(positions):
    t = positions.size
    tm = min(t, 2048)
    tq = tm // ROPE_PACK
    half = A_HEAD_DIM // 2
    inv = ROPE_THETA ** (-jnp.arange(0, A_HEAD_DIM, 2, dtype=F32) / A_HEAD_DIM)
    inv = jnp.tile(inv, ROPE_PACK)[None, :]
    pos = jnp.repeat(positions.reshape(t // ROPE_PACK, ROPE_PACK), half, axis=1)
    return pl.pallas_call(
        functools.partial(_rope_table_kernel, tq=tq),
        grid=(t // tm,),
        in_specs=[
            pl.BlockSpec((tq, LANES), lambda i: (i, 0)),
            pl.BlockSpec((1, LANES), lambda i: (0, 0)),
        ],
        out_specs=[pl.BlockSpec((tm, LANES), lambda i: (i, 0))] * 3,
        out_shape=[jax.ShapeDtypeStruct((t, LANES), F32)] * 3,
        name="rope_tables",
    )(pos, inv)


def _attn_proj_kernel(x_ref, g_ref, w_ref, cos_ref, sin_lo_ref, sin_hi_ref, *refs, tm):
    outs = refs[: 3 * A_N_GROUPS]
    h = _rms(x_ref[0], g_ref[...]).astype(BF16)
    cos = cos_ref[...]
    sin_lo = sin_lo_ref[...]
    sin_hi = sin_hi_ref[...]
    half = A_HEAD_DIM // 2
    dst = lax.broadcasted_iota(jnp.int32, (LANES, LANES), 0)
    src = lax.broadcasted_iota(jnp.int32, (LANES, LANES), 1)
    perms = {d: jnp.where(dst == (src % d) * (LANES // d) + src // d, 1.0, 0.0).astype(BF16)
             for _, d in A_GROUPS if d > 1}

    def project(part, g):
        c0 = (part * A_N_GROUPS + g) * A_GROUP_COLS
        return _dot(h, w_ref[:, c0:c0 + A_GROUP_COLS])

    def emit(part, g, y, slot):
        dil = A_GROUPS[g][1]
        o_ref = outs[g * 3 + part]
        for j in range(A_GROUP_COLS // LANES):
            lanes = slice(j * LANES, (j + 1) * LANES)
            yb = y[:, lanes]
            if part < 2:
                yb = (yb * cos + pltpu.roll(yb, LANES - half, axis=1) * sin_lo
                      + pltpu.roll(yb, half, axis=1) * sin_hi)
                if part == 0:
                    yb = yb * (A_HEAD_DIM ** -0.5 * LOG2_E)
            if dil == 1:
                o_ref[0, 0, :, lanes] = yb.astype(BF16)
            else:
                yb = yb.astype(BF16)
                w = LANES // dil
                z = [_dot(perms[dil], yb[grp * LANES:(grp + 1) * LANES]) for grp in range(tm // LANES)]
                for r in range(dil):
                    o_ref[0, r, :, lanes] = jnp.concatenate(
                        [zz[r * w:(r + 1) * w] for zz in z], axis=0).astype(BF16)

    order = [(0, 2), (1, 2), (0, 1), (1, 1), (2, 2), (2, 1), (0, 0), (1, 0), (2, 0)]
    y_next = project(*order[0])
    for idx, (part, g) in enumerate(order):
        y = y_next
        if idx + 1 < len(order):
            y_next = project(*order[idx + 1])
        emit(part, g, y, idx % 2)


def _attn_proj(x, g, w, layer, tables, tm=1024):
    bsz, seq, _ = x.shape
    nt = seq // tm
    out_shapes, out_specs = [], []
    for _, dil in A_GROUPS:
        for _ in range(3):
            out_shapes.append(jax.ShapeDtypeStruct((bsz, dil, seq // dil, A_GROUP_COLS), BF16))
            out_specs.append(pl.BlockSpec((1, dil, tm // dil, A_GROUP_COLS), lambda b, i: (b, 0, i, 0)))
    return pl.pallas_call(
        functools.partial(_attn_proj_kernel, tm=tm),
        grid=(bsz, nt),
        in_specs=[
            pl.BlockSpec((1, tm, D_MODEL), lambda b, i: (b, i, 0)),
            _resident((1, D_MODEL), layer[0]),
            _resident(w.shape[1:], layer[1]),
        ] + [pl.BlockSpec((tm, LANES), lambda b, i: (b * nt + i, 0))] * len(tables),
        out_specs=out_specs,
        out_shape=out_shapes,
        compiler_params=pltpu.CompilerParams(
            dimension_semantics=("parallel", "parallel"), vmem_limit_bytes=VMEM_LIMIT),
        name="attn_proj",
    )(x, g, w, *tables)


def _attn_core_kernel(*refs, seq):
    qkv = refs[: 3 * A_N_GROUPS]
    out_ref = refs[3 * A_N_GROUPS]
    o_scr, lse_scr, band_scr, causal_scr = refs[3 * A_N_GROUPS + 1:]
    span = A_SPAN
    lane = lax.broadcasted_iota(jnp.int32, (1, LANES), 1)
    v_is_a = lane < (LANES // 2)
    qk_is_a = v_is_a

    row = lax.broadcasted_iota(jnp.int32, (2 * span, 2 * span), 0) % span
    col = lax.broadcasted_iota(jnp.int32, (2 * span, 2 * span), 1)
    band_scr[...] = jnp.where((col >= row) & (col <= row + span), 0.0, NEG_INF)
    row1 = lax.broadcasted_iota(jnp.int32, (2 * span, span), 0) % span
    col1 = lax.broadcasted_iota(jnp.int32, (2 * span, span), 1)
    causal_scr[...] = jnp.where(col1 <= row1, 0.0, NEG_INF)

    def scores(g, r, n, first):
        q_ref, k_ref, _ = qkv[3 * g: 3 * g + 3]
        if first:
            qb = q_ref[0, r, pl.ds(0, span), :]
            kb = k_ref[0, r, pl.ds(0, span), :]
            bias = causal_scr[...]
        else:
            k0 = pl.multiple_of((n - 1) * span, span)
            qb = q_ref[0, r, pl.ds(k0 + span, span), :]
            kb = k_ref[0, r, pl.ds(k0, 2 * span), :]
            bias = band_scr[...]
        zero = jnp.zeros_like(qb)
        q2 = jnp.concatenate([jnp.where(qk_is_a, qb, zero), jnp.where(qk_is_a, zero, qb)], axis=0)
        return _dot_nt(q2, kb) + bias

    def finish(g, dil, r, n, first, s):
        v_ref = qkv[3 * g + 2]
        if first:
            vb = v_ref[0, r, pl.ds(0, span), :]
        else:
            vb = v_ref[0, r, pl.ds(pl.multiple_of((n - 1) * span, span), 2 * span), :]
        m_a = jnp.max(s[:span], axis=-1, keepdims=True)
        m_b = jnp.max(s[span:], axis=-1, keepdims=True)
        p = jnp.concatenate([jnp.exp2(s[:span] - m_a), jnp.exp2(s[span:] - m_b)], axis=1).astype(BF16)
        zero = jnp.zeros_like(vb)
        ind_a = jnp.broadcast_to(jnp.where(v_is_a, 1.0, 0.0), vb.shape).astype(BF16)
        ind_b = jnp.broadcast_to(jnp.where(v_is_a, 0.0, 1.0), vb.shape).astype(BF16)
        rhs = jnp.concatenate([
            jnp.concatenate([jnp.where(v_is_a, vb, zero), ind_a], axis=1),
            jnp.concatenate([jnp.where(v_is_a, zero, vb), ind_b], axis=1),
        ], axis=0)
        res = _dot(p, rhs)
        l = res[:, LANES:]
        o = res[:, :LANES] * (1.0 / l)
        lse = jnp.where(v_is_a, m_a, m_b) + jnp.log2(l)
        start = n * span * dil + r
        if dil == 1:
            rows = pl.ds(pl.multiple_of(start, span), span)
        else:
            rows = pl.ds(start, span, stride=dil)
        o_scr[g, rows, :] = o
        lse_scr[g, rows, :] = lse

    def blocks(g, dil, todo):
        pending = []
        for blk in todo:
            pending.append((blk, scores(g, *blk)))
            if len(pending) > A_SCORE_LOOKAHEAD:
                (r, n, first), s = pending.pop(0)
                finish(g, dil, r, n, first, s)
        for (r, n, first), s in pending:
            finish(g, dil, r, n, first, s)

    for g, (_, dil) in enumerate(A_GROUPS):
        nb = seq // dil // span
        if dil == 1:
            blocks(g, dil, [(0, 0, True)])
            per = A_BLOCKS_PER_BODY[g]
            def body(i, carry, g=g, dil=dil, per=per):
                blocks(g, dil, [(0, 1 + i * per + j, False) for j in range(per)])
                return carry
            lax.fori_loop(0, (nb - 1) // per, body, 0)
        else:
            streams = A_BLOCKS_PER_BODY[g] // nb
            def body(i, carry, g=g, dil=dil, nb=nb, streams=streams):
                blocks(g, dil, [(i * streams + j, n, n == 0)
                                for j in range(streams) for n in range(nb)])
                return carry
            lax.fori_loop(0, dil // streams, body, 0)

    lses = [lse_scr[g] for g in range(A_N_GROUPS)]
    mx = functools.reduce(jnp.maximum, lses)
    es = [jnp.exp2(v - mx) for v in lses]
    num = functools.reduce(lambda a, b: a + b, [es[g] * o_scr[g] for g in range(A_N_GROUPS)])
    den = functools.reduce(lambda a, b: a + b, es)
    out_ref[0] = (num * (1.0 / den)).astype(BF16)


def _attn_core(qkv, seq):
    bsz = qkv[0].shape[0]
    npair = A_GROUP_COLS // LANES
    in_specs = []
    for _, dil in A_GROUPS:
        for _ in range(3):
            in_specs.append(pl.BlockSpec((1, dil, seq // dil, LANES), lambda b, p: (b, 0, 0, p)))
    return pl.pallas_call(
        functools.partial(_attn_core_kernel, seq=seq),
        grid=(bsz, npair),
        in_specs=in_specs,
        out_specs=pl.BlockSpec((1, seq, LANES), lambda b, p: (b, 0, p)),
        out_shape=jax.ShapeDtypeStruct((bsz, seq, A_GROUP_COLS), BF16),
        scratch_shapes=[pltpu.VMEM((A_N_GROUPS, seq, LANES), F32),
                        pltpu.VMEM((A_N_GROUPS, seq, LANES), F32),
                        pltpu.VMEM((2 * A_SPAN, 2 * A_SPAN), F32),
                        pltpu.VMEM((2 * A_SPAN, A_SPAN), F32)],
        compiler_params=pltpu.CompilerParams(
            dimension_semantics=("parallel", "parallel"), vmem_limit_bytes=VMEM_LIMIT),
        name="attn_core",
    )(*qkv)


def _gla_proj_kernel(x_ref, g_ref, w_ref, wgd_ref, wgu_ref, bgu_ref, tri_ref,
                     q_ref, k_ref, v_ref, r_ref, b_ref, *, tm):
    h = _rms(x_ref[...], g_ref[...]).astype(BF16)
    gd = _dot(h, wgd_ref[...]).astype(BF16)
    z = _dot(gd, wgu_ref[...]) + bgu_ref[...]
    q_ref[...] = (_dot(h, w_ref[:, :B_KEY_DIM]) * (B_DK ** -0.5)).astype(BF16)
    k_ref[...] = _dot(h, w_ref[:, B_KEY_DIM:2 * B_KEY_DIM]).astype(BF16)
    log_sig = jnp.minimum(z, 0.0) - jnp.log(1.0 + jnp.exp(-jnp.abs(z)))
    glog = log_sig * (LOG2_E / B_GATE_NORMALIZER)
    hi = glog.astype(BF16)
    rem = glog - hi.astype(F32)
    mid = rem.astype(BF16)
    lo = (rem - mid.astype(F32)).astype(BF16)
    tri = tri_ref[...]
    for s in range(tm // B_SLAB):
        rows = slice(s * B_SLAB, (s + 1) * B_SLAB)
        b_ref[rows, :] = _dot(tri, hi[rows]) + _dot(tri, mid[rows]) + _dot(tri, lo[rows])
    c0 = 2 * B_KEY_DIM
    v_ref[...] = _dot(h, w_ref[:, c0:c0 + B_VAL_DIM]).astype(BF16)
    r = _dot(h, w_ref[:, c0 + B_VAL_DIM:c0 + 2 * B_VAL_DIM])
    r_ref[...] = (r * (1.0 / (1.0 + jnp.exp2(r * -LOG2_E)))).astype(BF16)


def _gla_proj(x2, g, w_main, w_gd, w_gu, b_gu, layers, tm=512):
    t = x2.shape[0]
    li, lj = layers
    row = lambda i: (i, 0)
    idx = jnp.arange(B_SLAB)
    tri = ((idx[:, None] // B_CHUNK == idx[None, :] // B_CHUNK)
           & (idx[None, :] <= idx[:, None])).astype(BF16)
    return pl.pallas_call(
        functools.partial(_gla_proj_kernel, tm=tm),
        grid=(t // tm,),
        in_specs=[
            pl.BlockSpec((tm, D_MODEL), row),
            _resident((1, D_MODEL), li),
            _resident(w_main.shape[1:], lj),
            _resident(w_gd.shape[1:], lj),
            _resident(w_gu.shape[1:], lj),
            _resident(b_gu.shape[1:], lj),
            _resident(tri.shape),
        ],
        out_specs=[
            pl.BlockSpec((tm, B_KEY_DIM), row),
            pl.BlockSpec((tm, B_KEY_DIM), row),
            pl.BlockSpec((tm, B_VAL_DIM), row),
            pl.BlockSpec((tm, B_VAL_DIM), row),
            pl.BlockSpec((tm, B_KEY_DIM), row),
        ],
        out_shape=[
            jax.ShapeDtypeStruct((t, B_KEY_DIM), BF16),
            jax.ShapeDtypeStruct((t, B_KEY_DIM), BF16),
            jax.ShapeDtypeStruct((t, B_VAL_DIM), BF16),
            jax.ShapeDtypeStruct((t, B_VAL_DIM), BF16),
            jax.ShapeDtypeStruct((t, B_KEY_DIM), F32),
        ],
        compiler_params=pltpu.CompilerParams(
            dimension_semantics=("parallel",), vmem_limit_bytes=VMEM_LIMIT),
        name="gla_proj",
    )(x2, g, w_main, w_gd, w_gu, b_gu, tri)


def _gla_core_kernel(q_ref, k_ref, v_ref, r_ref, b_ref, gn_ref, out_ref, *, seq, unroll, heads):
    c = B_CHUNK
    ri = lax.broadcasted_iota(jnp.int32, (c, c), 0)
    ci = lax.broadcasted_iota(jnp.int32, (c, c), 1)
    gn = gn_ref[...]

    def local(n, hh):
        base = n * c if isinstance(n, int) else pl.multiple_of(n * c, c)
        rows = pl.ds(base, c)
        kl = slice(hh * B_DK, (hh + 1) * B_DK)
        vl = slice(hh * B_DV, (hh + 1) * B_DV)
        bb = b_ref[0, rows, kl]
        b_mid = b_ref[0, pl.ds(base + c // 2 - 1, 1), kl]
        b_last = b_ref[0, pl.ds(base + c - 1, 1), kl]
        qf = q_ref[0, rows, kl].astype(F32)
        kf = k_ref[0, rows, kl].astype(F32)
        vb = v_ref[0, rows, vl]
        x_mid = bb - b_mid
        qe = (qf * jnp.exp2(x_mid)).astype(BF16)
        ke = (kf * jnp.exp2(-x_mid)).astype(BF16)
        a = jnp.where(ci <= ri, _dot_nt(qe, ke), 0.0).astype(BF16)
        qi = (qf * jnp.exp2(bb)).astype(BF16)
        ks = (kf * jnp.exp2(b_last - bb)).astype(BF16)
        return rows, vl, a, vb, qi, _dot_tn(vb, ks), jnp.exp2(b_last)

    def finish(st, rows, vl, a, vb, qi, kv, decay):
        o = _dot(a, vb) + _dot_nt(qi, st.astype(BF16))
        o = _rms(o, gn)
        out_ref[0, rows, vl] = (o * r_ref[0, rows, vl].astype(F32)).astype(BF16)
        return st * decay + kv

    def body(i, sts):
        sts = list(sts)
        pending = []
        for j in range(unroll):
            for hh in range(heads):
                pending.append((hh, local(i * unroll + j, hh)))
                if len(pending) > B_LOCAL_LOOKAHEAD * heads:
                    hq, part = pending.pop(0)
                    sts[hq] = finish(sts[hq], *part)
        for hq, part in pending:
            sts[hq] = finish(sts[hq], *part)
        return tuple(sts)

    init = tuple(jnp.zeros((B_DV, B_DK), F32) for _ in range(heads))
    trips = seq // c // unroll
    if trips == 1:
        body(0, init)
    else:
        lax.fori_loop(0, trips, body, init)


def _gla_core(q, k, v, r, b, g_norm, layer, seq, unroll=32, heads=2):
    bsz = q.shape[0]
    hblock = lambda width: pl.BlockSpec((1, seq, heads * width), lambda b, h: (b, 0, h))
    return pl.pallas_call(
        functools.partial(_gla_core_kernel, seq=seq, unroll=unroll, heads=heads),
        grid=(bsz, B_HEADS // heads),
        in_specs=[hblock(B_DK), hblock(B_DK), hblock(B_DV), hblock(B_DV), hblock(B_DK),
                  _resident((1, B_DV), layer)],
        out_specs=hblock(B_DV),
        out_shape=jax.ShapeDtypeStruct((bsz, seq, B_VAL_DIM), BF16),
        compiler_params=pltpu.CompilerParams(
            dimension_semantics=("parallel", "parallel"), vmem_limit_bytes=VMEM_LIMIT),
        name="gla_core",
    )(q, k, v, r, b, g_norm)


def _ffn_kernel(x_ref, mix_ref, wo_ref, g_ref, win_ref, cw_ref, cb_ref, wd_ref, gf_ref,
                out_ref, carry_scr, act_scr, *, tm, nsub, final):
    i = pl.program_id(1)

    @pl.when(i == 0)
    def _():
        carry_scr[...] = jnp.zeros_like(carry_scr)

    ts = tm // nsub
    subs = [slice(s * ts, (s + 1) * ts) for s in range(nsub)]
    x1 = [x_ref[0, rows, :] + _dot(mix_ref[0, rows, :], wo_ref[...]) for rows in subs]
    h = [_rms(v, g_ref[...]).astype(BF16) for v in x1]
    row = lax.broadcasted_iota(jnp.int32, (ts, FFN_CHUNK), 0)
    for s, rows in enumerate(subs):
        for c in range(FFN_NCHUNK):
            cols = slice(c * FFN_CHUNK, (c + 1) * FFN_CHUNK)
            a = _dot(h[s], win_ref[:, cols])
            u = _dot(h[s], win_ref[:, FFN_DIM + c * FFN_CHUNK:FFN_DIM + (c + 1) * FFN_CHUNK])
            prev2 = carry_scr[6:7, cols]
            prev1 = carry_scr[7:8, cols]
            a1 = jnp.where(row == 0, prev1, pltpu.roll(a, 1, axis=0))
            a2 = jnp.where(row == 0, prev2, jnp.where(row == 1, prev1, pltpu.roll(a, 2, axis=0)))
            carry_scr[:, cols] = a[ts - 8:, :]
            conv = (a * cw_ref[2:3, cols] + a1 * cw_ref[1:2, cols] + a2 * cw_ref[0:1, cols]
                    + cb_ref[:, cols])
            act = conv * (1.0 / (1.0 + jnp.exp2(conv * -LOG2_E))) * u
            act_scr[rows, cols] = act.astype(BF16)
        y = x1[s] + _dot(act_scr[rows, :], wd_ref[...])
        if final:
            y = _rms(y, gf_ref[...])
        out_ref[0, rows, :] = y


def _ffn(x, mix, w_o, g, w_in, conv_w, conv_b, w_down, g_final, layers, final, tm=1024, nsub=2):
    bsz, seq, _ = x.shape
    kmix = mix.shape[-1]
    li, lj = layers
    return pl.pallas_call(
        functools.partial(_ffn_kernel, tm=tm, nsub=nsub, final=final),
        grid=(bsz, seq // tm),
        in_specs=[
            pl.BlockSpec((1, tm, D_MODEL), lambda b, i: (b, i, 0)),
            pl.BlockSpec((1, tm, kmix), lambda b, i: (b, i, 0)),
            _resident(w_o.shape[1:], lj),
            _resident((1, D_MODEL), li),
            _resident(w_in.shape[1:], li),
            _resident(conv_w.shape[1:], li),
            _resident(conv_b.shape[1:], li),
            _resident(w_down.shape[1:], li),
            _resident((1, D_MODEL)),
        ],
        out_specs=pl.BlockSpec((1, tm, D_MODEL), lambda b, i: (b, i, 0)),
        out_shape=jax.ShapeDtypeStruct(x.shape, F32),
        scratch_shapes=[pltpu.VMEM((8, FFN_DIM), F32), pltpu.VMEM((tm, FFN_DIM), BF16)],
        compiler_params=pltpu.CompilerParams(
            dimension_semantics=("arbitrary", "arbitrary"), vmem_limit_bytes=VMEM_LIMIT),
        name="ffn",
    )(x, mix, w_o, g, w_in, conv_w, conv_b, w_down, g_final)


def kernel(x, positions, norm_mix, norm_ffn, a_w_qkv, a_w_o, b_w_in, b_w_gate_up,
           b_b_gate_up, b_g_norm, b_w_o, f_w_in, f_conv_w, f_conv_b, f_w_down, norm_final):
    bsz, seq, _ = x.shape
    depth = norm_mix.shape[0]
    g_mix = norm_mix[:, None, :]
    g_ffn = norm_ffn[:, None, :]
    g_final = norm_final[None, :]
    a_w = a_w_qkv.astype(BF16)
    a_wo = a_w_o.astype(BF16)
    ncore = 2 * B_KEY_DIM + 2 * B_VAL_DIM
    b_w = b_w_in[:, :, :ncore].astype(BF16)
    b_wgd = jnp.pad(b_w_in[:, :, ncore:], ((0, 0), (0, 0), (0, LANES - B_GATE_RANK))).astype(BF16)
    b_wgu = jnp.pad(b_w_gate_up, ((0, 0), (0, LANES - B_GATE_RANK), (0, 0))).astype(BF16)
    b_bgu = b_b_gate_up[:, None, :]
    b_gn = b_g_norm[:, None, :]
    b_wo = b_w_o.astype(BF16)
    f_win = f_w_in.astype(BF16)
    f_wd = f_w_down.astype(BF16)
    f_cb = f_conv_b[:, None, :]

    tables = _rope_tables(positions)
    for i in range(depth):
        j = i // 2
        if i % 2 == 0:
            qkv = _attn_proj(x, g_mix, a_w, (i, j), tables)
            mix = _attn_core(qkv, seq)
            w_o = a_wo
        else:
            q, k, v, r, b = _gla_proj(x.reshape(bsz * seq, D_MODEL), g_mix, b_w, b_wgd, b_wgu,
                                      b_bgu, (i, j))
            shp = lambda t: t.reshape(bsz, seq, t.shape[-1])
            mix = _gla_core(shp(q), shp(k), shp(v), shp(r), shp(b), b_gn, j, seq)
            w_o = b_wo
        x = _ffn(x, mix, w_o, g_ffn, f_win, f_conv_w, f_cb, f_wd, g_final, (i, j),
                 final=(i == depth - 1))
    return x
```
